```python
import jax, jax.numpy as jnp
from jax import lax
import numpy as np

D_MODEL = 1024
BATCH = 4
SEQ = 4096
DEPTH = 4

DN_HEADS = 8
DN_HEAD_DIM = 128
DN_WIDTH = DN_HEADS * DN_HEAD_DIM
DN_CONV = 4
DN_CHUNK = 64
SWA_Q_HEADS = 16
SWA_KV_HEADS = 2
SWA_HEAD_DIM = 64
SWA_GROUP = SWA_Q_HEADS // SWA_KV_HEADS
SWA_WIDTH = SWA_Q_HEADS * SWA_HEAD_DIM
SWA_KV_WIDTH = SWA_KV_HEADS * SWA_HEAD_DIM
WINDOW = 128
SWA_BLOCK = 128
ROPE_THETA = 500000.0
ROPE_DIM = SWA_HEAD_DIM // 4
D_FF = 2816
FFN_CONV = 3
EPS = 1e-6
IN_SIZES = (3 * DN_WIDTH, DN_WIDTH, DN_HEADS, DN_HEADS, SWA_WIDTH, SWA_KV_WIDTH, SWA_KV_WIDTH, D_MODEL, D_MODEL)
IN_TOTAL = 4 * DN_WIDTH + 2 * DN_HEADS + SWA_WIDTH + 2 * SWA_KV_WIDTH + 2 * D_MODEL

kernel_name = "hybrid_gdn_swa_sink_convffn_adaln"


def _split_columns(t, sizes):
    idx, acc = [], 0
    for s in sizes[:-1]:
        acc += s
        idx.append(acc)
    return jnp.split(t, idx, axis=-1)


def rms_norm(x, w):
    xf = x.astype(jnp.float32)
    y = xf * lax.rsqrt(jnp.mean(xf * xf, axis=-1, keepdims=True) + EPS)
    return (y * w.astype(jnp.float32)).astype(x.dtype)


def l2_norm(x):
    xf = x.astype(jnp.float32)
    return (xf * lax.rsqrt(jnp.sum(xf * xf, axis=-1, keepdims=True) + EPS)).astype(x.dtype)


def causal_dwconv(x, w):
    K, C = w.shape
    return lax.conv_general_dilated(
        x, w[:, None, :].astype(x.dtype), window_strides=(1,), padding=[(K - 1, 0)],
        dimension_numbers=('NWC', 'WIO', 'NWC'), feature_group_count=C)


def partial_rope(x, pos):
    half = ROPE_DIM // 2
    inv = jnp.power(ROPE_THETA, -jnp.arange(half, dtype=jnp.float32) / half)
    ang = pos.astype(jnp.float32)[..., None] * inv
    cos, sin = jnp.cos(ang)[:, :, None, :], jnp.sin(ang)[:, :, None, :]
    xr = x[..., :ROPE_DIM].astype(jnp.float32)
    x1, x2 = xr[..., :half], xr[..., half:]
    rot = jnp.concatenate([x1 * cos - x2 * sin, x2 * cos + x1 * sin], axis=-1).astype(x.dtype)
    return jnp.concatenate([rot, x[..., ROPE_DIM:]], axis=-1)


def gated_delta_rule_chunked(q, k, v, g, beta):
    B, T, H, Dk = q.shape
    Dv = v.shape[-1]
    C = DN_CHUNK
    N = T // C
    f32 = jnp.float32

    def chunks(t):
        return t.astype(f32).reshape(B, N, C, H, -1).transpose(0, 3, 1, 2, 4)

    qc = chunks(q) * (Dk ** -0.5)
    kc, vc = chunks(k), chunks(v)
    gc = jnp.cumsum(g.astype(f32).reshape(B, N, C, H).transpose(0, 3, 1, 2), axis=-1)
    bc = beta.astype(f32).reshape(B, N, C, H).transpose(0, 3, 1, 2)
    tri_incl = jnp.tril(jnp.ones((C, C), dtype=bool))
    tri_strict = jnp.tril(jnp.ones((C, C), dtype=bool), -1)
    decay = jnp.exp(jnp.where(tri_incl, gc[..., :, None] - gc[..., None, :], -jnp.inf))
    k_beta = kc * bc[..., None]
    L = jnp.where(tri_strict, jnp.einsum('bhnid,bhnjd->bhnij', k_beta, kc) * decay, 0.0)
    A = L + jnp.eye(C, dtype=f32)
    rhs = jnp.concatenate([vc * bc[..., None], k_beta * jnp.exp(gc)[..., None]], axis=-1)
    sol = lax.linalg.triangular_solve(A, rhs, left_side=True, lower=True, unit_diagonal=True)
    u, w = sol[..., :Dv], sol[..., Dv:]
    qk = jnp.einsum('bhnid,bhnjd->bhnij', qc, kc) * decay
    q_dec = qc * jnp.exp(gc)[..., None]
    g_last = gc[..., -1]
    k_dec = kc * jnp.exp(g_last[..., None] - gc)[..., None]

    def step(S, xs):
        qd, kd, u_i, w_i, qk_i, gl = xs
        v_new = u_i - jnp.einsum('bhcd,bhde->bhce', w_i, S)
        o = jnp.einsum('bhcd,bhde->bhce', qd, S) + jnp.einsum('bhij,bhje->bhie', qk_i, v_new)
        S = S * jnp.exp(gl)[..., None, None] + jnp.einsum('bhcd,bhce->bhde', kd, v_new)
        return S, o

    xs = tuple(jnp.moveaxis(t, 2, 0) for t in (q_dec, k_dec, u, w, qk, g_last))
    S0 = jnp.zeros((B, H, Dk, Dv), f32)
    _, o = lax.scan(step, S0, xs)
    return o.transpose(1, 0, 3, 2, 4).reshape(B, T, H, Dv).astype(v.dtype)


def swa_sink_attention(q, k, v, sinks):
    B, T, Hq, D = q.shape
    nb = T // SWA_BLOCK
    f32 = jnp.float32
    qb = q.astype(f32).reshape(B, nb, SWA_BLOCK, SWA_KV_HEADS, SWA_GROUP, D)

    def band(t):
        tb = t.astype(f32).reshape(B, nb, SWA_BLOCK, SWA_KV_HEADS, D)
        prev = jnp.pad(tb, ((0, 0), (1, 0), (0, 0), (0, 0), (0, 0)))[:, :-1]
        return jnp.concatenate([prev, tb], axis=2)

    kb, vb = band(k), band(v)
    s = jnp.einsum('bnqhgd,bnkhd->bnhgqk', qb, kb) * (D ** -0.5)
    qi = jnp.arange(SWA_BLOCK)[:, None]
    kj = jnp.arange(2 * SWA_BLOCK)[None, :]
    rel = qi + SWA_BLOCK - kj
    blk = jnp.arange(nb)[:, None, None]
    mask = (rel >= 0) & (rel < WINDOW) & (blk * SWA_BLOCK + kj >= SWA_BLOCK)
    s = jnp.where(mask[None, :, None, None], s, -jnp.inf)
    sink = sinks.astype(f32).reshape(1, 1, SWA_KV_HEADS, SWA_GROUP, 1, 1)
    m = jnp.maximum(jnp.max(s, axis=-1, keepdims=True), sink)
    p = jnp.exp(s - m)
    denom = jnp.sum(p, axis=-1, keepdims=True) + jnp.exp(sink - m)
    o = jnp.einsum('bnhgqk,bnkhd->bnqhgd', p / denom, vb)
    return o.reshape(B, T, Hq * D).astype(q.dtype)


def setup_inputs(seed: int = 0) -> dict:
    key = jax.random.key(seed)
    ks = jax.random.split(key, 24)
    nrm = jax.random.normal
    f32 = jnp.float32
    Lr = DEPTH
    x = nrm(ks[0], (BATCH, SEQ, D_MODEL), f32)
    c = nrm(ks[1], (BATCH, D_MODEL), f32)
    positions = (jax.random.randint(ks[2], (BATCH, 1), 0, 2048, dtype=jnp.int32)
                 + jnp.arange(SEQ, dtype=jnp.int32)[None, :])
    w_ada = nrm(ks[3], (Lr, D_MODEL, 6 * D_MODEL), f32) * D_MODEL ** -0.5
    b_ada = nrm(ks[4], (Lr, 6 * D_MODEL), f32) * 0.02
    norm_mix = 1.0 + 0.02 * nrm(ks[5], (Lr, D_MODEL), f32)
    w_in = nrm(ks[6], (Lr, D_MODEL, IN_TOTAL), f32) * D_MODEL ** -0.5
    dn_conv = nrm(ks[7], (Lr, DN_CONV, 3 * DN_WIDTH), f32) * DN_CONV ** -0.5
    dn_a_log = jnp.log(jax.random.uniform(ks[8], (Lr, DN_HEADS), f32, 1.0, 16.0))
    dt = jnp.exp(jax.random.uniform(ks[9], (Lr, DN_HEADS), f32, np.log(1e-3), np.log(1e-1)))
    dn_dt_bias = jnp.log(jnp.expm1(dt))
    dn_norm = 1.0 + 0.02 * nrm(ks[10], (Lr, DN_HEAD_DIM), f32)
    w_dn_out = nrm(ks[11], (Lr, DN_WIDTH, D_MODEL), f32) * DN_WIDTH ** -0.5
    swa_q_norm = 1.0 + 0.02 * nrm(ks[12], (Lr, SWA_HEAD_DIM), f32)
    swa_k_norm = 1.0 + 0.02 * nrm(ks[13], (Lr, SWA_HEAD_DIM), f32)
    swa_sinks = nrm(ks[14], (Lr, SWA_Q_HEADS), f32)
    w_swa_out = nrm(ks[15], (Lr, SWA_WIDTH, D_MODEL), f32) * SWA_WIDTH ** -0.5
    w_o = nrm(ks[16], (Lr, D_MODEL, D_MODEL), f32) * D_MODEL ** -0.5
    norm_ffn = 1.0 + 0.02 * nrm(ks[17], (Lr, D_MODEL), f32)
    w_up = nrm(ks[18], (Lr, D_MODEL, 2 * D_FF), f32) * D_MODEL ** -0.5
    ffn_conv = nrm(ks[19], (Lr, FFN_CONV, D_FF), f32) * FFN_CONV ** -0.5
    ffn_conv_b = nrm(ks[20], (Lr, D_FF), f32) * 0.02
    w_down = nrm(ks[21], (Lr, D_FF, D_MODEL), f32) * D_FF ** -0.5
    return {"x": x, "c": c, "positions": positions, "w_ada": w_ada, "b_ada": b_ada,
            "norm_mix": norm_mix, "w_in": w_in, "dn_conv": dn_conv, "dn_a_log": dn_a_log,
            "dn_dt_bias": dn_dt_bias, "dn_norm": dn_norm, "w_dn_out": w_dn_out,
            "swa_q_norm": swa_q_norm, "swa_k_norm": swa_k_norm, "swa_sinks": swa_sinks,
            "w_swa_out": w_swa_out, "w_o": w_o, "norm_ffn": norm_ffn, "w_up": w_up,
            "ffn_conv": ffn_conv, "ffn_conv_b": ffn_conv_b, "w_down": w_down}


def reference(x, c, positions, w_ada, b_ada, norm_mix, w_in, dn_conv, dn_a_log, dn_dt_bias,
              dn_norm, w_dn_out, swa_q_norm, swa_k_norm, swa_sinks, w_swa_out, w_o,
              norm_ffn, w_up, ffn_conv, ffn_conv_b, w_down):
    B, T, _ = x.shape
    c_act = jax.nn.silu(c)
    for l in range(DEPTH):
        mod = c_act @ w_ada[l] + b_ada[l]
        sh1, sc1, gt1, sh2, sc2, gt2 = [m[:, None, :] for m in jnp.split(mod, 6, axis=-1)]

        h = rms_norm(x, norm_mix[l]) * (1.0 + sc1) + sh1
        proj = h @ w_in[l]
        dn_qkv, dn_z, dn_a, dn_b, sw_q, sw_k, sw_v, gate_a, gate_b = _split_columns(proj, IN_SIZES)

        dn_qkv = jax.nn.silu(causal_dwconv(dn_qkv, dn_conv[l]))
        dq, dk, dv = [t.reshape(B, T, DN_HEADS, DN_HEAD_DIM) for t in jnp.split(dn_qkv, 3, axis=-1)]
        dq, dk = l2_norm(dq), l2_norm(dk)
        g = -jnp.exp(dn_a_log[l].astype(jnp.float32)) * jax.nn.softplus(
            dn_a.astype(jnp.float32) + dn_dt_bias[l].astype(jnp.float32))
        beta = jax.nn.sigmoid(dn_b.astype(jnp.float32))
        o_dn = gated_delta_rule_chunked(dq, dk, dv, g, beta)
        o_dn = rms_norm(o_dn, dn_norm[l]) * jax.nn.silu(dn_z.reshape(B, T, DN_HEADS, DN_HEAD_DIM))
        y_a = o_dn.reshape(B, T, DN_WIDTH) @ w_dn_out[l]

        sq = partial_rope(rms_norm(sw_q.reshape(B, T, SWA_Q_HEADS, SWA_HEAD_DIM), swa_q_norm[l]), positions)
        sk = partial_rope(rms_norm(sw_k.reshape(B, T, SWA_KV_HEADS, SWA_HEAD_DIM), swa_k_norm[l]), positions)
        sv = sw_v.reshape(B, T, SWA_KV_HEADS, SWA_HEAD_DIM)
        y_b = swa_sink_attention(sq, sk, sv, swa_sinks[l]) @ w_swa_out[l]

        merged = jax.nn.sigmoid(gate_a) * y_a + jax.nn.sigmoid(gate_b) * y_b
        x = x + gt1 * (merged @ w_o[l])

        h = rms_norm(x, norm_ffn[l]) * (1.0 + sc2) + sh2
        up_act, up_lin = jnp.split(h @ w_up[l], 2, axis=-1)
        up_act = causal_dwconv(up_act, ffn_conv[l]) + ffn_conv_b[l]
        x = x + gt2 * ((jax.nn.silu(up_act) * up_lin) @ w_down[l])
    return x
```

```python
import functools

import numpy as np
import jax
import jax.numpy as jnp
from jax import lax
from jax.experimental import pallas as pl
from jax.experimental.pallas import tpu as pltpu

F32 = jnp.float32
BF16 = jnp.bfloat16

D_MODEL = 1024
DN_HEADS = 8
DN_HEAD_DIM = 128
DN_WIDTH = DN_HEADS * DN_HEAD_DIM
DN_CONV = 4
DN_CHUNK = 64
SWA_Q_HEADS = 16
SWA_KV_HEADS = 2
SWA_HEAD_DIM = 64
SWA_WIDTH = SWA_Q_HEADS * SWA_HEAD_DIM
SWA_KV_WIDTH = SWA_KV_HEADS * SWA_HEAD_DIM
SWA_BLOCK = 128
ROPE_THETA = 500000.0
ROPE_DIM = SWA_HEAD_DIM // 4
ROPE_HALF = ROPE_DIM // 2
D_FF = 2816
FFN_CONV = 3
EPS = 1e-6

LANES = 128
SUBLANES = 8
VMEM_LIMIT = 56 * 1024 * 1024

COL_Q, COL_K, COL_V, COL_Z = 0, 1024, 2048, 3072
COL_SWQ, COL_GA, COL_GB = 4096, 5120, 6144
COL_SWK, COL_SWV, COL_AB = 7168, 7296, 7424
IN_PACKED = 7680
NEG_BIG = -1e30


def _sigmoid(x):
    return 1.0 / (1.0 + jnp.exp(-x))


def _silu(x):
    return x * _sigmoid(x)


def _mm(a, b):
    return jnp.dot(a.astype(BF16), b.astype(BF16), preferred_element_type=F32)


def _mm_nt(a, b):
    return lax.dot_general(a.astype(BF16), b.astype(BF16), (((1,), (1,)), ((), ())),
                           preferred_element_type=F32)


def _params(*sem):
    return pltpu.CompilerParams(dimension_semantics=sem, vmem_limit_bytes=VMEM_LIMIT)


def _ada_kernel(c_ref, w_ref, b_ref, o_ref):
    ca = _silu(c_ref[...])
    o_ref[0] = jnp.dot(ca, w_ref[0], precision=lax.Precision.HIGHEST,
                       preferred_element_type=F32) + b_ref[0]


def _ada_mod(c, w_ada, b_ada):
    depth = w_ada.shape[0]
    batch = c.shape[0]
    n_out = w_ada.shape[2]
    tn = 1536
    c_pad = jnp.zeros((SUBLANES, D_MODEL), F32).at[:batch].set(c)
    return pl.pallas_call(
        _ada_kernel,
        grid=(depth, n_out // tn),
        in_specs=[pl.BlockSpec((SUBLANES, D_MODEL), lambda l, j: (0, 0)),
                  pl.BlockSpec((1, D_MODEL, tn), lambda l, j: (l, 0, j)),
                  pl.BlockSpec((1, 1, tn), lambda l, j: (l, 0, j))],
        out_specs=pl.BlockSpec((1, SUBLANES, tn), lambda l, j: (l, 0, j)),
        out_shape=jax.ShapeDtypeStruct((depth, SUBLANES, n_out), F32),
        compiler_params=_params("arbitrary", "arbitrary"),
        name="ada_mod",
    )(c_pad, w_ada, b_ada.reshape(depth, 1, n_out))


def _rope_kernel(pos_ref, inv_ref, sgn_ref, cos_ref, sin_ref):
    ang = pos_ref[...] * inv_ref[...]
    on = sgn_ref[...] != 0.0
    cos_ref[...] = jnp.where(on, jnp.cos(ang), 1.0)
    sin_ref[...] = jnp.sin(ang) * sgn_ref[...]


def _rope_tables(positions):
    m = positions.size
    tm = min(2048, m)
    lane = np.arange(LANES) % SWA_HEAD_DIM
    inv = np.where(lane < ROPE_DIM,
                   np.power(ROPE_THETA, -(lane % ROPE_HALF).astype(np.float64) / ROPE_HALF), 0.0)
    sgn = np.where(lane < ROPE_HALF, -1.0, np.where(lane < ROPE_DIM, 1.0, 0.0))
    pos = positions.astype(F32).reshape(m, 1)
    return pl.pallas_call(
        _rope_kernel,
        grid=(m // tm,),
        in_specs=[pl.BlockSpec((tm, 1), lambda i: (i, 0)),
                  pl.BlockSpec((1, LANES), lambda i: (0, 0)),
                  pl.BlockSpec((1, LANES), lambda i: (0, 0))],
        out_specs=[pl.BlockSpec((tm, LANES), lambda i: (i, 0)),
                   pl.BlockSpec((tm, LANES), lambda i: (i, 0))],
        out_shape=[jax.ShapeDtypeStruct((m, LANES), F32)] * 2,
        compiler_params=_params("arbitrary"),
        name="rope_tables",
    )(pos, jnp.asarray(inv, F32).reshape(1, LANES), jnp.asarray(sgn, F32).reshape(1, LANES))


def _norm_mod(x, nw, sc, sh):
    ms = jnp.mean(x * x, axis=-1, keepdims=True)
    return (x * lax.rsqrt(ms + EPS) * nw) * (1.0 + sc) + sh


def _inproj_kernel(x_ref, nw_ref, sc_ref, sh_ref, w_ref, alog_ref, dtb_ref,
                   proj_ref, gate_ref, h_scr, *, n_col_tiles, ab_off):
    j = pl.program_id(1)

    @pl.when(j == 0)
    def _():
        h_scr[...] = _norm_mod(x_ref[...], nw_ref[...], sc_ref[0], sh_ref[0]).astype(BF16)

    acc = jnp.dot(h_scr[...], w_ref[...], preferred_element_type=F32)
    proj_ref[...] = acc.astype(BF16)

    @pl.when(j == n_col_tiles - 1)
    def _():
        ab = acc[:, ab_off:ab_off + LANES]
        z = ab + dtb_ref[...]
        softplus = jnp.maximum(z, 0.0) + jnp.log(1.0 + jnp.exp(-jnp.abs(z)))
        g = -jnp.exp(alog_ref[...]) * softplus
        lane = lax.broadcasted_iota(jnp.int32, ab.shape, 1)
        gate_ref[...] = jnp.where(lane < DN_HEADS, g, _sigmoid(ab))


def _inproj(x, nw, mod, w, alog, dtb, seq):
    m = x.shape[0]
    tm, tn = 1024, 768
    nj = IN_PACKED // tn
    tiles_per_seq = seq // tm
    kern = functools.partial(_inproj_kernel, n_col_tiles=nj, ab_off=COL_AB - (nj - 1) * tn)
    return pl.pallas_call(
        kern,
        grid=(m // tm, nj),
        in_specs=[pl.BlockSpec((tm, D_MODEL), lambda i, j: (i, 0)),
                  pl.BlockSpec((1, D_MODEL), lambda i, j: (0, 0)),
                  pl.BlockSpec((1, 1, D_MODEL), lambda i, j: ((i // tiles_per_seq) * 6 + 1, 0, 0)),
                  pl.BlockSpec((1, 1, D_MODEL), lambda i, j: ((i // tiles_per_seq) * 6 + 0, 0, 0)),
                  pl.BlockSpec((D_MODEL, tn), lambda i, j: (0, j)),
                  pl.BlockSpec((1, LANES), lambda i, j: (0, 0)),
                  pl.BlockSpec((1, LANES), lambda i, j: (0, 0))],
        out_specs=[pl.BlockSpec((tm, tn), lambda i, j: (i, j)),
                   pl.BlockSpec((tm, LANES), lambda i, j: (i, 0))],
        out_shape=[jax.ShapeDtypeStruct((m, IN_PACKED), BF16),
                   jax.ShapeDtypeStruct((m, LANES), F32)],
        scratch_shapes=[pltpu.VMEM((tm, D_MODEL), BF16)],
        compiler_params=_params("arbitrary", "arbitrary"),
        name="inproj",
    )(x, nw, mod, mod, w, alog, dtb)


INV_BLOCK = 16


def _inv_masks(c):
    row = lax.broadcasted_iota(jnp.int32, (c, c), 0)
    col = lax.broadcasted_iota(jnp.int32, (c, c), 1)
    masks = []
    b = INV_BLOCK
    same = (row // b) == (col // b)
    masks.append(same)
    while b < c:
        b *= 2
        same2 = (row // b) == (col // b)
        masks.append(jnp.logical_and(same2, jnp.logical_not(same)))
        same = same2
    return masks


def _inv_unit_lower(l_strict, eye, masks):
    d = jnp.where(masks[0], l_strict, 0.0)
    t = eye - d
    mpow = _mm(d, d)
    n_fac = int(np.log2(INV_BLOCK)) - 1
    for i in range(n_fac):
        t = t + _mm(t, mpow)
        if i < n_fac - 1:
            mpow = _mm(mpow, mpow)
    for off_mask in masks[1:]:
        t = t - _mm(t, _mm(jnp.where(off_mask, l_strict, 0.0), t))
    return t


def _deltanet_kernel(q_ref, k_ref, v_ref, z_ref, gate_ref, cw_ref, nw_ref, o_ref,
                     s_scr, xbuf, act_scr, *, tb):
    t = pl.program_id(1)
    c = DN_CHUNK
    halo = SUBLANES

    @pl.when(t == 0)
    def _():
        s_scr[...] = jnp.zeros_like(s_scr)
        xbuf[:, 0:halo, :] = jnp.zeros((3, halo, DN_WIDTH), F32)

    for idx, ref in enumerate((q_ref, k_ref, v_ref)):
        xbuf[idx, halo:halo + tb, :] = ref[...].astype(F32)
        w = cw_ref[:, idx * DN_WIDTH:(idx + 1) * DN_WIDTH]
        y = w[DN_CONV - 1:DN_CONV, :] * xbuf[idx, halo:halo + tb, :]
        for s in range(1, DN_CONV):
            y = y + w[DN_CONV - 1 - s:DN_CONV - s, :] * xbuf[idx, halo - s:halo - s + tb, :]
        xbuf[idx, 0:halo, :] = xbuf[idx, tb:tb + halo, :]
        act_scr[idx] = _silu(y)

    row = lax.broadcasted_iota(jnp.int32, (c, c), 0)
    col = lax.broadcasted_iota(jnp.int32, (c, c), 1)
    tri_incl = row >= col
    tri_strict = row > col
    eye = jnp.where(row == col, 1.0, 0.0).astype(F32)
    tri_f = jnp.where(tri_incl, 1.0, 0.0).astype(F32)
    inv_masks = _inv_masks(c)
    nw = nw_ref[...]

    def chunk_body(ci, carry):
        r0 = pl.multiple_of(ci * c, c)
        gates = gate_ref[pl.ds(r0, c), :]
        gcum = jnp.dot(tri_f, gates, precision=lax.Precision.HIGHEST,
                       preferred_element_type=F32)
        gcum_t = gcum.T
        for h in range(DN_HEADS):
            sl = slice(h * DN_HEAD_DIM, (h + 1) * DN_HEAD_DIM)
            qh = act_scr[0, pl.ds(r0, c), sl]
            kh = act_scr[1, pl.ds(r0, c), sl]
            vh = act_scr[2, pl.ds(r0, c), sl]
            qh = qh * lax.rsqrt(jnp.sum(qh * qh, axis=-1, keepdims=True) + EPS) * (DN_HEAD_DIM ** -0.5)
            kh = kh * lax.rsqrt(jnp.sum(kh * kh, axis=-1, keepdims=True) + EPS)
            gcol = gcum[:, h:h + 1]
            grow = gcum_t[h:h + 1, :]
            beta = gates[:, DN_HEADS + h:DN_HEADS + h + 1]
            glast = gcum[c - 1:c, h:h + 1]
            decay = jnp.where(tri_incl, jnp.exp(gcol - grow), 0.0)
            kb = kh * beta
            a = _mm_nt(jnp.concatenate([kb, qh], axis=0), kh)
            l_strict = jnp.where(tri_strict, a[:c] * decay, 0.0)
            qk = a[c:] * decay
            tinv = _inv_unit_lower(l_strict, eye, inv_masks)
            eg = jnp.exp(gcol)
            uw = _mm(tinv, jnp.concatenate([vh * beta, kb * eg], axis=1))
            u, w = uw[:, :DN_HEAD_DIM], uw[:, DN_HEAD_DIM:]
            s_h = s_scr[h]
            ws = _mm(jnp.concatenate([w, qh * eg], axis=0), s_h)
            v_new = u - ws[:c]
            o = ws[c:] + _mm(qk, v_new)
            kd = kh * jnp.exp(glast - gcol)
            s_scr[h] = s_h * jnp.exp(glast) + _mm(kd.T, v_new)
            o = o * lax.rsqrt(jnp.mean(o * o, axis=-1, keepdims=True) + EPS) * nw
            zh = z_ref[pl.ds(r0, c), sl].astype(F32)
            o_ref[pl.ds(r0, c), sl] = (o * _silu(zh)).astype(BF16)
        return carry

    lax.fori_loop(0, tb // c, chunk_body, 0)


def _deltanet(proj, gates, conv_w, norm_w, batch, seq):
    m = proj.shape[0]
    tb = 256
    nt = seq // tb
    kern = functools.partial(_deltanet_kernel, tb=tb)

    def col(cb):
        return pl.BlockSpec((tb, DN_WIDTH), lambda b, t: (b * nt + t, cb))

    return pl.pallas_call(
        kern,
        grid=(batch, nt),
        in_specs=[col(COL_Q // DN_WIDTH), col(COL_K // DN_WIDTH), col(COL_V // DN_WIDTH),
                  col(COL_Z // DN_WIDTH),
                  pl.BlockSpec((tb, LANES), lambda b, t: (b * nt + t, 0)),
                  pl.BlockSpec((DN_CONV, 3 * DN_WIDTH), lambda b, t: (0, 0)),
                  pl.BlockSpec((1, DN_HEAD_DIM), lambda b, t: (0, 0))],
        out_specs=pl.BlockSpec((tb, DN_WIDTH), lambda b, t: (b * nt + t, 0)),
        out_shape=jax.ShapeDtypeStruct((m, DN_WIDTH), BF16),
        scratch_shapes=[pltpu.VMEM((DN_HEADS, DN_HEAD_DIM, DN_HEAD_DIM), F32),
                        pltpu.VMEM((3, tb + SUBLANES, DN_WIDTH), F32),
                        pltpu.VMEM((3, tb, DN_WIDTH), F32)],
        compiler_params=_params("arbitrary", "arbitrary"),
        name="deltanet",
    )(proj, proj, proj, proj, gates, conv_w, norm_w)


def _head_mean_sq(x, block_ones):
    x2 = x * x
    hi = x2.astype(BF16)
    lo = (x2 - hi.astype(F32)).astype(BF16)
    outs = []
    w = block_ones.shape[0]
    for c0 in range(0, x.shape[1], w):
        outs.append(jnp.dot(hi[:, c0:c0 + w], block_ones, preferred_element_type=F32)
                    + jnp.dot(lo[:, c0:c0 + w], block_ones, preferred_element_type=F32))
    ms = outs[0] if len(outs) == 1 else jnp.concatenate(outs, axis=1)
    return ms * (1.0 / SWA_HEAD_DIM)


def _norm_rope(x, nw, cos_f, sin_f, block_ones):
    width = x.shape[1]
    y = x * lax.rsqrt(_head_mean_sq(x, block_ones) + EPS) * nw
    lane = lax.broadcasted_iota(jnp.int32, y.shape, 1) % SWA_HEAD_DIM
    partner = jnp.where(lane < ROPE_HALF,
                        pltpu.roll(y, width - ROPE_HALF, axis=1),
                        pltpu.roll(y, ROPE_HALF, axis=1))
    return y * cos_f + partner * sin_f


def _swa_kernel(sink_ref, q_ref, k_ref, v_ref, cos_ref, sin_ref, qn_ref, kn_ref, o_ref,
                kprev, vprev):
    n = pl.program_id(1)
    blk = SWA_BLOCK

    @pl.when(n == 0)
    def _():
        kprev[...] = jnp.zeros_like(kprev)
        vprev[...] = jnp.zeros_like(vprev)

    r = lax.broadcasted_iota(jnp.int32, (2 * LANES, 2 * LANES), 0) // SWA_HEAD_DIM
    cc = lax.broadcasted_iota(jnp.int32, (2 * LANES, 2 * LANES), 1) // SWA_HEAD_DIM
    block_ones = jnp.where(r == cc, 1.0, 0.0).astype(BF16)

    cos_f = cos_ref[...]
    sin_f = sin_ref[...]
    q = _norm_rope(q_ref[...].astype(F32), qn_ref[...],
                   jnp.tile(cos_f, (1, SWA_WIDTH // LANES)), jnp.tile(sin_f, (1, SWA_WIDTH // LANES)),
                   block_ones) * (SWA_HEAD_DIM ** -0.5)
    k_cur = _norm_rope(k_ref[...].astype(F32), kn_ref[...], cos_f, sin_f,
                       block_ones[:LANES, :LANES]).astype(BF16)
    v_cur = v_ref[...]

    k_band = jnp.concatenate([kprev[...], k_cur], axis=0)
    v_band = jnp.concatenate([vprev[...], v_cur], axis=0)
    kprev[...] = k_cur
    vprev[...] = v_cur

    low = lax.broadcasted_iota(jnp.int32, (2 * blk, LANES), 1) < SWA_HEAD_DIM

    def split_heads(band):
        bf = band.astype(F32)
        swap = pltpu.roll(bf, SWA_HEAD_DIM, axis=1)
        even = (jnp.where(low, bf, 0.0).astype(BF16), jnp.where(low, swap, 0.0).astype(BF16))
        odd = (jnp.where(low, 0.0, swap).astype(BF16), jnp.where(low, 0.0, bf).astype(BF16))
        return even, odd

    k_even, k_odd = split_heads(k_band)
    v_even, v_odd = split_heads(v_band)

    qi = lax.broadcasted_iota(jnp.int32, (blk, 2 * blk), 0)
    kj = lax.broadcasted_iota(jnp.int32, (blk, 2 * blk), 1)
    rel = qi + blk - kj
    bias = jnp.where(rel >= 0, jnp.where(rel < blk, 0.0, NEG_BIG), NEG_BIG)
    prev_penalty = jnp.where(n > 0, 0.0, NEG_BIG)
    bias = bias + jnp.where(kj < blk, prev_penalty, 0.0)

    qb = q.astype(BF16)
    group = SWA_Q_HEADS // SWA_KV_HEADS
    for pair in range(SWA_Q_HEADS // 2):
        kv = (2 * pair) // group
        q2 = qb[:, pair * LANES:(pair + 1) * LANES]
        acc = None
        for half, (kop, vop) in enumerate(((k_even[kv], v_even[kv]), (k_odd[kv], v_odd[kv]))):
            sink = sink_ref[2 * pair + half]
            s = _mm_nt(q2, kop) + bias
            mx = jnp.maximum(jnp.max(s, axis=-1, keepdims=True), sink)
            p = jnp.exp(s - mx)
            denom = jnp.sum(p, axis=-1, keepdims=True) + jnp.exp(sink - mx)
            contrib = _mm(p, vop) * (1.0 / denom)
            acc = contrib if acc is None else acc + contrib
        o_ref[:, pair * LANES:(pair + 1) * LANES] = acc.astype(BF16)


def _swa(proj, cos_t, sin_t, sinks, qn, kn, batch, seq):
    m = proj.shape[0]
    blk = SWA_BLOCK
    nb = seq // blk
    row = lambda b, n: b * nb + n
    return pl.pallas_call(
        _swa_kernel,
        grid=(batch, nb),
        in_specs=[pl.BlockSpec(memory_space=pltpu.SMEM),
                  pl.BlockSpec((blk, SWA_WIDTH), lambda b, n: (row(b, n), COL_SWQ // SWA_WIDTH)),
                  pl.BlockSpec((blk, SWA_KV_WIDTH), lambda b, n: (row(b, n), COL_SWK // SWA_KV_WIDTH)),
                  pl.BlockSpec((blk, SWA_KV_WIDTH), lambda b, n: (row(b, n), COL_SWV // SWA_KV_WIDTH)),
                  pl.BlockSpec((blk, LANES), lambda b, n: (row(b, n), 0)),
                  pl.BlockSpec((blk, LANES), lambda b, n: (row(b, n), 0)),
                  pl.BlockSpec((1, SWA_WIDTH), lambda b, n: (0, 0)),
                  pl.BlockSpec((1, SWA_KV_WIDTH), lambda b, n: (0, 0))],
        out_specs=pl.BlockSpec((blk, SWA_WIDTH), lambda b, n: (row(b, n), 0)),
        out_shape=jax.ShapeDtypeStruct((m, SWA_WIDTH), BF16),
        scratch_shapes=[pltpu.VMEM((blk, SWA_KV_WIDTH), BF16),
                        pltpu.VMEM((blk, SWA_KV_WIDTH), BF16)],
        compiler_params=_params("arbitrary", "arbitrary"),
        name="swa",
    )(sinks, proj, proj, proj, cos_t, sin_t, qn, kn)


def _merge_kernel(x_ref, odn_ref, osw_ref, ga_ref, gb_ref, gt_ref, wdn_ref, wsw_ref, wo_ref, o_ref):
    ya = jnp.dot(odn_ref[...], wdn_ref[...], preferred_element_type=F32)
    yb = jnp.dot(osw_ref[...], wsw_ref[...], preferred_element_type=F32)
    merged = _sigmoid(ga_ref[...].astype(F32)) * ya + _sigmoid(gb_ref[...].astype(F32)) * yb
    out = jnp.dot(merged.astype(BF16), wo_ref[...], preferred_element_type=F32)
    o_ref[...] = x_ref[...] + gt_ref[0] * out


def _merge(x, o_dn, o_sw, proj, mod, w_dn, w_sw, w_o, seq):
    m = x.shape[0]
    tm = 512
    tiles_per_seq = seq // tm
    tok = lambda cb: pl.BlockSpec((tm, D_MODEL), lambda i: (i, cb))
    wfull = pl.BlockSpec((D_MODEL, D_MODEL), lambda i: (0, 0))
    return pl.pallas_call(
        _merge_kernel,
        grid=(m // tm,),
        in_specs=[tok(0), tok(0), tok(0), tok(COL_GA // D_MODEL), tok(COL_GB // D_MODEL),
                  pl.BlockSpec((1, 1, D_MODEL), lambda i: ((i // tiles_per_seq) * 6 + 2, 0, 0)),
                  wfull, wfull, wfull],
        out_specs=tok(0),
        out_shape=jax.ShapeDtypeStruct((m, D_MODEL), F32),
        input_output_aliases={0: 0},
        compiler_params=_params("arbitrary"),
        name="merge_out",
    )(x, o_dn, o_sw, proj, proj, mod, w_dn, w_sw, w_o)


def _ffn_kernel(x_ref, nw_ref, sc_ref, sh_ref, gt_ref, wa_ref, wl_ref, cw_ref, cb_ref, wd_ref,
                o_ref, h_scr, acc_scr, abuf, halo_scr, *, tm, tiles_per_seq, n_ff_tiles):
    i = pl.program_id(0)
    j = pl.program_id(1)
    halo = SUBLANES

    @pl.when(j == 0)
    def _():
        h_scr[...] = _norm_mod(x_ref[...], nw_ref[...], sc_ref[0], sh_ref[0]).astype(BF16)

    h = h_scr[...]
    a = jnp.dot(h, wa_ref[...], preferred_element_type=F32)
    lin = jnp.dot(h, wl_ref[...], preferred_element_type=F32)

    first = (i % tiles_per_seq) == 0
    prev = halo_scr[j]
    abuf[0:halo, :] = jnp.where(first, jnp.zeros_like(prev), prev)
    abuf[halo:halo + tm, :] = a
    halo_scr[j] = a[tm - halo:tm, :]
    w = cw_ref[...]
    y = w[FFN_CONV - 1:FFN_CONV, :] * a + cb_ref[...]
    for s in range(1, FFN_CONV):
        y = y + w[FFN_CONV - 1 - s:FFN_CONV - s, :] * abuf[halo - s:halo - s + tm, :]
    act = (_silu(y) * lin).astype(BF16)
    part = jnp.dot(act, wd_ref[...], preferred_element_type=F32)

    @pl.when(j == 0)
    def _():
        acc_scr[...] = part

    @pl.when(j > 0)
    def _():
        acc_scr[...] += part

    @pl.when(j == n_ff_tiles - 1)
    def _():
        o_ref[...] = x_ref[...] + gt_ref[0] * acc_scr[...]


def _ffn(x, nw, mod, w_up, conv_w, conv_b, w_down, seq):
    m = x.shape[0]
    tm = 512
    n_ff = 2
    fc = D_FF // n_ff
    tiles_per_seq = seq // tm
    kern = functools.partial(_ffn_kernel, tm=tm, tiles_per_seq=tiles_per_seq, n_ff_tiles=n_ff)
    modspec = lambda k: pl.BlockSpec((1, 1, D_MODEL), lambda i, j: ((i // tiles_per_seq) * 6 + k, 0, 0))
    return pl.pallas_call(
        kern,
        grid=(m // tm, n_ff),
        in_specs=[pl.BlockSpec((tm, D_MODEL), lambda i, j: (i, 0)),
                  pl.BlockSpec((1, D_MODEL), lambda i, j: (0, 0)),
                  modspec(4), modspec(3), modspec(5),
                  pl.BlockSpec((D_MODEL, fc), lambda i, j: (0, j)),
                  pl.BlockSpec((D_MODEL, fc), lambda i, j: (0, n_ff + j)),
                  pl.BlockSpec((FFN_CONV, fc), lambda i, j: (0, j)),
                  pl.BlockSpec((1, fc), lambda i, j: (0, j)),
                  pl.BlockSpec((fc, D_MODEL), lambda i, j: (j, 0))],
        out_specs=pl.BlockSpec((tm, D_MODEL), lambda i, j: (i, 0)),
        out_shape=jax.ShapeDtypeStruct((m, D_MODEL), F32),
        scratch_shapes=[pltpu.VMEM((tm, D_MODEL), BF16),
                        pltpu.VMEM((tm, D_MODEL), F32),
                        pltpu.VMEM((tm + SUBLANES, fc), F32),
                        pltpu.VMEM((n_ff, SUBLANES, fc), F32)],
        input_output_aliases={0: 0},
        compiler_params=_params("arbitrary", "arbitrary"),
        name="ffn",
    )(x, nw, mod, mod, mod, w_up, w_up, conv_w, conv_b, w_down)


def _pack_w_in(w_in):
    depth = w_in.shape[0]
    o_z, o_a = 3 * DN_WIDTH, 4 * DN_WIDTH
    o_swq = o_a + 2 * DN_HEADS
    o_swk = o_swq + SWA_WIDTH
    o_swv = o_swk + SWA_KV_WIDTH
    o_ga = o_swv + SWA_KV_WIDTH
    o_gb = o_ga + D_MODEL
    del o_z
    pad = jnp.zeros((depth, D_MODEL, IN_PACKED - COL_AB - 2 * DN_HEADS), w_in.dtype)
    packed = jnp.concatenate(
        [w_in[:, :, :o_a], w_in[:, :, o_swq:o_swk], w_in[:, :, o_ga:o_gb], w_in[:, :, o_gb:],
         w_in[:, :, o_swk:o_swv], w_in[:, :, o_swv:o_ga], w_in[:, :, o_a:o_swq], pad], axis=2)
    return packed.astype(BF16)


def _lane_row(v):
    depth, n = v.shape
    return jnp.zeros((depth, 1, LANES), F32).at[:, 0, :n].set(v.astype(F32))


def kernel(x, c, positions, w_ada, b_ada, norm_mix, w_in, dn_conv, dn_a_log, dn_dt_bias, dn_norm,
           w_dn_out, swa_q_norm, swa_k_norm, swa_sinks, w_swa_out, w_o, norm_ffn, w_up, ffn_conv,
           ffn_conv_b, w_down):
    batch, seq, _ = x.shape
    depth = w_ada.shape[0]
    m = batch * seq

    mod_all = _ada_mod(c, w_ada, b_ada)
    cos_t, sin_t = _rope_tables(positions)

    w_in_p = _pack_w_in(w_in)
    w_dn_b, w_sw_b, w_o_b = w_dn_out.astype(BF16), w_swa_out.astype(BF16), w_o.astype(BF16)
    w_up_b, w_down_b = w_up.astype(BF16), w_down.astype(BF16)
    alog = _lane_row(dn_a_log)
    dtb = _lane_row(dn_dt_bias)
    qn = jnp.tile(swa_q_norm, (1, SWA_Q_HEADS)).reshape(depth, 1, SWA_WIDTH)
    kn = jnp.tile(swa_k_norm, (1, SWA_KV_HEADS)).reshape(depth, 1, SWA_KV_WIDTH)

    xf = x.reshape(m, D_MODEL)
    for l in range(depth):
        mod = mod_all[l].reshape(SUBLANES * 6, 1, D_MODEL)
        proj, gates = _inproj(xf, norm_mix[l].reshape(1, D_MODEL), mod, w_in_p[l], alog[l], dtb[l], seq)
        o_dn = _deltanet(proj, gates, dn_conv[l], dn_norm[l].reshape(1, DN_HEAD_DIM), batch, seq)
        o_sw = _swa(proj, cos_t, sin_t, swa_sinks[l], qn[l], kn[l], batch, seq)
        xf = _merge(xf, o_dn, o_sw, proj, mod, w_dn_b[l], w_sw_b[l], w_o_b[l], seq)
        xf = _ffn(xf, norm_ffn[l].reshape(1, D_MODEL), mod, w_up_b[l], ffn_conv[l],
                  ffn_conv_b[l].reshape(1, D_FF), w_down_b[l], seq)
    return xf.reshape(batch, seq, D_MODEL)
```

```python
import functools

import numpy as np
import jax
import jax.numpy as jnp
from jax import lax
from jax.experimental import pallas as pl
from jax.experimental.pallas import tpu as pltpu

F32 = jnp.float32
BF16 = jnp.bfloat16

D_MODEL = 1024
DN_HEADS = 8
DN_HEAD_DIM = 128
DN_WIDTH = DN_HEADS * DN_HEAD_DIM
DN_CONV = 4
DN_CHUNK = 64
SWA_Q_HEADS = 16
SWA_KV_HEADS = 2
SWA_HEAD_DIM = 64
SWA_WIDTH = SWA_Q_HEADS * SWA_HEAD_DIM
SWA_KV_WIDTH = SWA_KV_HEADS * SWA_HEAD_DIM
SWA_BLOCK = 128
ROPE_THETA = 500000.0
ROPE_DIM = SWA_HEAD_DIM // 4
ROPE_HALF = ROPE_DIM // 2
D_FF = 2816
FFN_CONV = 3
EPS = 1e-6

LANES = 128
SUBLANES = 8
VMEM_LIMIT = 56 * 1024 * 1024

COL_Q, COL_K, COL_V, COL_Z = 0, 1024, 2048, 3072
COL_SWQ, COL_GA, COL_GB = 4096, 5120, 6144
COL_SWK, COL_SWV, COL_AB = 7168, 7296, 7424
IN_PACKED = 7680
NEG_BIG = -1e30


def _sigmoid(x):
    return 1.0 / (1.0 + jnp.exp(-x))


def _silu(x):
    return x * _sigmoid(x)


def _mm(a, b):
    return jnp.dot(a.astype(BF16), b.astype(BF16), preferred_element_type=F32)


def _mm_nt(a, b):
    return lax.dot_general(a.astype(BF16), b.astype(BF16), (((1,), (1,)), ((), ())),
                           preferred_element_type=F32)


def _params(*sem):
    return pltpu.CompilerParams(dimension_semantics=sem, vmem_limit_bytes=VMEM_LIMIT)


def _ada_kernel(c_ref, w_ref, b_ref, o_ref):
    ca = _silu(c_ref[...])
    o_ref[0] = jnp.dot(ca, w_ref[0], precision=lax.Precision.HIGHEST,
                       preferred_element_type=F32) + b_ref[0]


def _ada_mod(c, w_ada, b_ada):
    depth = w_ada.shape[0]
    batch = c.shape[0]
    n_out = w_ada.shape[2]
    tn = 1536
    c_pad = jnp.zeros((SUBLANES, D_MODEL), F32).at[:batch].set(c)
    return pl.pallas_call(
        _ada_kernel,
        grid=(depth, n_out // tn),
        in_specs=[pl.BlockSpec((SUBLANES, D_MODEL), lambda l, j: (0, 0)),
                  pl.BlockSpec((1, D_MODEL, tn), lambda l, j: (l, 0, j)),
                  pl.BlockSpec((1, 1, tn), lambda l, j: (l, 0, j))],
        out_specs=pl.BlockSpec((1, SUBLANES, tn), lambda l, j: (l, 0, j)),
        out_shape=jax.ShapeDtypeStruct((depth, SUBLANES, n_out), F32),
        compiler_params=_params("arbitrary", "arbitrary"),
        name="ada_mod",
    )(c_pad, w_ada, b_ada.reshape(depth, 1, n_out))


def _rope_kernel(pos_ref, inv_ref, sgn_ref, cos_ref, sin_ref):
    ang = pos_ref[...] * inv_ref[...]
    on = sgn_ref[...] != 0.0
    cos_ref[...] = jnp.where(on, jnp.cos(ang), 1.0)
    sin_ref[...] = jnp.sin(ang) * sgn_ref[...]


def _rope_tables(positions):
    m = positions.size
    tm = min(2048, m)
    lane = np.arange(LANES) % SWA_HEAD_DIM
    inv = np.where(lane < ROPE_DIM,
                   np.power(ROPE_THETA, -(lane % ROPE_HALF).astype(np.float64) / ROPE_HALF), 0.0)
    sgn = np.where(lane < ROPE_HALF, -1.0, np.where(lane < ROPE_DIM, 1.0, 0.0))
    pos = positions.astype(F32).reshape(m, 1)
    return pl.pallas_call(
        _rope_kernel,
        grid=(m // tm,),
        in_specs=[pl.BlockSpec((tm, 1), lambda i: (i, 0)),
                  pl.BlockSpec((1, LANES), lambda i: (0, 0)),
                  pl.BlockSpec((1, LANES), lambda i: (0, 0))],
        out_specs=[pl.BlockSpec((tm, LANES), lambda i: (i, 0)),
                   pl.BlockSpec((tm, LANES), lambda i: (i, 0))],
        out_shape=[jax.ShapeDtypeStruct((m, LANES), F32)] * 2,
        compiler_params=_params("arbitrary"),
        name="rope_tables",
    )(pos, jnp.asarray(inv, F32).reshape(1, LANES), jnp.asarray(sgn, F32).reshape(1, LANES))


def _norm_mod(x, nw, sc, sh):
    ms = jnp.mean(x * x, axis=-1, keepdims=True)
    return (x * lax.rsqrt(ms + EPS) * nw) * (1.0 + sc) + sh


def _inproj_kernel(x_ref, nw_ref, sc_ref, sh_ref, w_ref, alog_ref, dtb_ref,
                   proj_ref, gate_ref, h_scr, *, n_col_tiles, ab_off):
    j = pl.program_id(1)

    @pl.when(j == 0)
    def _():
        h_scr[...] = _norm_mod(x_ref[...], nw_ref[...], sc_ref[0], sh_ref[0]).astype(BF16)

    acc = jnp.dot(h_scr[...], w_ref[...], preferred_element_type=F32)
    proj_ref[...] = acc.astype(BF16)

    @pl.when(j == n_col_tiles - 1)
    def _():
        ab = acc[:, ab_off:ab_off + LANES]
        z = ab + dtb_ref[...]
        softplus = jnp.maximum(z, 0.0) + jnp.log(1.0 + jnp.exp(-jnp.abs(z)))
        g = -jnp.exp(alog_ref[...]) * softplus
        lane = lax.broadcasted_iota(jnp.int32, ab.shape, 1)
        gate_ref[...] = jnp.where(lane < DN_HEADS, g, _sigmoid(ab))


def _inproj(x, nw, mod, w, alog, dtb, seq):
    m = x.shape[0]
    tm, tn = 1024, 768
    nj = IN_PACKED // tn
    tiles_per_seq = seq // tm
    kern = functools.partial(_inproj_kernel, n_col_tiles=nj, ab_off=COL_AB - (nj - 1) * tn)
    return pl.pallas_call(
        kern,
        grid=(m // tm, nj),
        in_specs=[pl.BlockSpec((tm, D_MODEL), lambda i, j: (i, 0)),
                  pl.BlockSpec((1, D_MODEL), lambda i, j: (0, 0)),
                  pl.BlockSpec((1, 1, D_MODEL), lambda i, j: ((i // tiles_per_seq) * 6 + 1, 0, 0)),
                  pl.BlockSpec((1, 1, D_MODEL), lambda i, j: ((i // tiles_per_seq) * 6 + 0, 0, 0)),
                  pl.BlockSpec((D_MODEL, tn), lambda i, j: (0, j)),
                  pl.BlockSpec((1, LANES), lambda i, j: (0, 0)),
                  pl.BlockSpec((1, LANES), lambda i, j: (0, 0))],
        out_specs=[pl.BlockSpec((tm, tn), lambda i, j: (i, j)),
                   pl.BlockSpec((tm, LANES), lambda i, j: (i, 0))],
        out_shape=[jax.ShapeDtypeStruct((m, IN_PACKED), BF16),
                   jax.ShapeDtypeStruct((m, LANES), F32)],
        scratch_shapes=[pltpu.VMEM((tm, D_MODEL), BF16)],
        compiler_params=_params("arbitrary", "arbitrary"),
        name="inproj",
    )(x, nw, mod, mod, w, alog, dtb)


DN_BLOCK = 256
INV_BLOCK = 16
DN_PREV_ROWS = 16
DN_HEAD_GROUP = 4
M_INCL, M_STRICT, M_DIAG, M_EYE, M_OFF0 = 0, 1, 2, 3, 4


def _dn_masks():
    r = np.arange(DN_BLOCK)[:, None]
    c = np.arange(DN_BLOCK)[None, :]
    same = lambda b: (r // b) == (c // b)
    chunk = same(DN_CHUNK)
    masks = [chunk & (r >= c), chunk & (r > c), same(INV_BLOCK), r == c]
    b = INV_BLOCK
    while b < DN_CHUNK:
        masks.append(same(2 * b) & ~same(b))
        b *= 2
    return np.stack(masks).astype(np.float32)


def _inv_unit_lower(l_strict, mask_ref):
    dot = functools.partial(jnp.dot, preferred_element_type=F32)
    ds = [l * mask_ref[M_DIAG] for l in l_strict]
    ts = [mask_ref[M_EYE] - d for d in ds]
    dbs = [d.astype(BF16) for d in ds]
    mpows = [dot(db, db) for db in dbs]
    n_fac = int(np.log2(INV_BLOCK)) - 1
    for i in range(n_fac):
        mbs = [m.astype(BF16) for m in mpows]
        ts = [t + dot(t.astype(BF16), mb) for t, mb in zip(ts, mbs)]
        if i < n_fac - 1:
            mpows = [dot(mb, mb) for mb in mbs]
    n_levels = int(np.log2(DN_CHUNK // INV_BLOCK))
    for lvl in range(n_levels):
        tbs = [t.astype(BF16) for t in ts]
        inner = [dot((l * mask_ref[M_OFF0 + lvl]).astype(BF16), tb).astype(BF16)
                 for l, tb in zip(l_strict, tbs)]
        ts = [t - dot(tb, inn) for t, tb, inn in zip(ts, tbs, inner)]
    return ts


def _dn_intra_kernel(q_ref, k_ref, v_ref, qp_ref, kp_ref, vp_ref, gate_ref, cw_ref, mask_ref,
                     u_ref, w_ref, qd_ref, kd_ref, qk_ref, egl_ref, xbuf, act_scr, *, blocks_per_seq):
    i = pl.program_id(0)
    tb, c, halo = DN_BLOCK, DN_CHUNK, DN_PREV_ROWS
    n_chunks = tb // c
    keep_prev = jnp.where((i % blocks_per_seq) == 0, 0.0, 1.0)

    for idx, (ref, pref) in enumerate(((q_ref, qp_ref), (k_ref, kp_ref), (v_ref, vp_ref))):
        xbuf[0:halo, :] = pref[...].astype(F32) * keep_prev
        xbuf[halo:halo + tb, :] = ref[...].astype(F32)
        w = cw_ref[:, idx * DN_WIDTH:(idx + 1) * DN_WIDTH]
        y = w[DN_CONV - 1:DN_CONV, :] * xbuf[halo:halo + tb, :]
        for s in range(1, DN_CONV):
            y = y + w[DN_CONV - 1 - s:DN_CONV - s, :] * xbuf[halo - s:halo - s + tb, :]
        act_scr[idx] = _silu(y)

    gates = gate_ref[...]
    gcum = jnp.dot(mask_ref[M_INCL], gates, precision=lax.Precision.HIGHEST,
                   preferred_element_type=F32)
    gcum_t = gcum.T
    glast = jnp.concatenate(
        [jnp.broadcast_to(gcum[ci * c + c - 1:ci * c + c, :], (c, LANES)) for ci in range(n_chunks)], axis=0)
    e_cum = jnp.exp(gcum)
    e_rem = jnp.exp(glast - gcum)
    for ci in range(n_chunks):
        gl = gcum_t[0:DN_HEADS, ci * c + c - 1:ci * c + c]
        egl_ref[ci * DN_HEADS:(ci + 1) * DN_HEADS, :] = jnp.broadcast_to(jnp.exp(gl), (DN_HEADS, LANES))

    for h0 in range(0, DN_HEADS, DN_HEAD_GROUP):
        heads = range(h0, h0 + DN_HEAD_GROUP)
        sls = [slice(h * DN_HEAD_DIM, (h + 1) * DN_HEAD_DIM) for h in heads]
        rhs, ls = [], []
        for h, sl in zip(heads, sls):
            qh, kh, vh = act_scr[0, :, sl], act_scr[1, :, sl], act_scr[2, :, sl]
            qh = qh * (lax.rsqrt(jnp.sum(qh * qh, axis=-1, keepdims=True) + EPS) * (DN_HEAD_DIM ** -0.5))
            kh = kh * lax.rsqrt(jnp.sum(kh * kh, axis=-1, keepdims=True) + EPS)
            beta = gates[:, DN_HEADS + h:DN_HEADS + h + 1]
            eg = e_cum[:, h:h + 1]
            kb = kh * beta
            qd_ref[:, sl] = (qh * eg).astype(BF16)
            kd_ref[:, sl] = (kh * e_rem[:, h:h + 1]).astype(BF16)
            rhs.append(jnp.concatenate([vh * beta, kb * eg], axis=1).astype(BF16))
            a = _mm_nt(jnp.concatenate([kb, qh], axis=0), kh)
            decay = jnp.exp(jnp.minimum(gcum[:, h:h + 1] - gcum_t[h:h + 1, :], 0.0)) * mask_ref[M_INCL]
            ls.append(a[:tb] * (decay * mask_ref[M_STRICT]))
            qk = a[tb:] * decay
            qk_ref[:, sl] = jnp.concatenate(
                [qk[ci * c:(ci + 1) * c, (ci // 2) * LANES:(ci // 2 + 1) * LANES] for ci in range(n_chunks)],
                axis=0).astype(BF16)
        tinvs = _inv_unit_lower(ls, mask_ref)
        for sl, tinv, r in zip(sls, tinvs, rhs):
            uw = jnp.dot(tinv.astype(BF16), r, preferred_element_type=F32)
            u_ref[:, sl] = uw[:, :DN_HEAD_DIM]
            w_ref[:, sl] = uw[:, DN_HEAD_DIM:].astype(BF16)


def _dn_scan_kernel(u_ref, w_ref, qd_ref, kd_ref, qk_ref, z_ref, egl_ref, nw_ref, o_ref, s_scr, *, tb):
    t = pl.program_id(1)
    c = DN_CHUNK

    @pl.when(t == 0)
    def _():
        s_scr[...] = jnp.zeros_like(s_scr)

    nw = nw_ref[...]
    zeros_v = jnp.zeros((c, DN_HEAD_DIM), BF16)

    def pair_body(pi, carry):
        for par in range(2):
            ci = 2 * pi + par
            r0 = pl.multiple_of(ci * c, c)
            heads = range(DN_HEADS)
            sls = [slice(h * DN_HEAD_DIM, (h + 1) * DN_HEAD_DIM) for h in heads]
            rows = pl.ds(r0, c)
            s_old = [s_scr[h] for h in heads]
            ws = [jnp.dot(jnp.concatenate([w_ref[rows, sl], qd_ref[rows, sl]], axis=0),
                          s.astype(BF16), preferred_element_type=F32) for sl, s in zip(sls, s_old)]
            vbs = [(u_ref[rows, sl] - w_s[:c]).astype(BF16) for sl, w_s in zip(sls, ws)]
            kd_ts = [kd_ref[rows, sl].astype(F32).T.astype(BF16) for sl in sls]
            for h, sl, s, vb, kd_t in zip(heads, sls, s_old, vbs, kd_ts):
                egl = egl_ref[pl.ds(ci * DN_HEADS + h, 1), :]
                s_scr[h] = s * egl + jnp.dot(kd_t, vb, preferred_element_type=F32)
            for sl, w_s, vb in zip(sls, ws, vbs):
                v_pad = jnp.concatenate([vb, zeros_v] if par == 0 else [zeros_v, vb], axis=0)
                o = w_s[c:] + jnp.dot(qk_ref[rows, sl], v_pad, preferred_element_type=F32)
                o = o * lax.rsqrt(jnp.mean(o * o, axis=-1, keepdims=True) + EPS) * nw
                zh = z_ref[rows, sl].astype(F32)
                o_ref[rows, sl] = (o * _silu(zh)).astype(BF16)
        return carry

    lax.fori_loop(0, tb // (2 * c), pair_body, 0)


def _deltanet(proj, gates, conv_w, norm_w, batch, seq):
    m = proj.shape[0]
    tb = DN_BLOCK
    masks = jnp.asarray(_dn_masks())
    prev_per_blk = tb // DN_PREV_ROWS
    cur = lambda cb: pl.BlockSpec((tb, DN_WIDTH), lambda i: (i, cb))
    prev = lambda cb: pl.BlockSpec((DN_PREV_ROWS, DN_WIDTH),
                                   lambda i: (jnp.maximum(i * prev_per_blk - 1, 0), cb))
    tok = pl.BlockSpec((tb, DN_WIDTH), lambda i: (i, 0))
    egl_rows = (tb // DN_CHUNK) * DN_HEADS
    bf_out = jax.ShapeDtypeStruct((m, DN_WIDTH), BF16)
    u, w, qd, kd, qk, egl = pl.pallas_call(
        functools.partial(_dn_intra_kernel, blocks_per_seq=seq // tb),
        grid=(m // tb,),
        in_specs=[cur(COL_Q // DN_WIDTH), cur(COL_K // DN_WIDTH), cur(COL_V // DN_WIDTH),
                  prev(COL_Q // DN_WIDTH), prev(COL_K // DN_WIDTH), prev(COL_V // DN_WIDTH),
                  pl.BlockSpec((tb, LANES), lambda i: (i, 0)),
                  pl.BlockSpec((DN_CONV, 3 * DN_WIDTH), lambda i: (0, 0)),
                  pl.BlockSpec(masks.shape, lambda i: (0, 0, 0))],
        out_specs=[tok, tok, tok, tok, tok, pl.BlockSpec((egl_rows, LANES), lambda i: (i, 0))],
        out_shape=[jax.ShapeDtypeStruct((m, DN_WIDTH), F32), bf_out, bf_out, bf_out, bf_out,
                   jax.ShapeDtypeStruct((m // DN_CHUNK * DN_HEADS, LANES), F32)],
        scratch_shapes=[pltpu.VMEM((tb + DN_PREV_ROWS, DN_WIDTH), F32),
                        pltpu.VMEM((3, tb, DN_WIDTH), F32)],
        compiler_params=_params("parallel"),
        name="dn_intra",
    )(proj, proj, proj, proj, proj, proj, gates, conv_w, masks)

    ts = 256
    nt = seq // ts
    blk = lambda cb: pl.BlockSpec((ts, DN_WIDTH), lambda b, t: (b * nt + t, cb))
    return pl.pallas_call(
        functools.partial(_dn_scan_kernel, tb=ts),
        grid=(batch, nt),
        in_specs=[blk(0), blk(0), blk(0), blk(0), blk(0), blk(COL_Z // DN_WIDTH),
                  pl.BlockSpec((ts // DN_CHUNK * DN_HEADS, LANES), lambda b, t: (b * nt + t, 0)),
                  pl.BlockSpec((1, DN_HEAD_DIM), lambda b, t: (0, 0))],
        out_specs=blk(0),
        out_shape=bf_out,
        scratch_shapes=[pltpu.VMEM((DN_HEADS, DN_HEAD_DIM, DN_HEAD_DIM), F32)],
        compiler_params=_params("arbitrary", "arbitrary"),
        name="dn_scan",
    )(u, w, qd, kd, qk, proj, egl, norm_w)


def _head_mean_sq(x, block_ones):
    x2 = x * x
    hi = x2.astype(BF16)
    lo = (x2 - hi.astype(F32)).astype(BF16)
    outs = []
    w = block_ones.shape[0]
    for c0 in range(0, x.shape[1], w):
        outs.append(jnp.dot(hi[:, c0:c0 + w], block_ones, preferred_element_type=F32)
                    + jnp.dot(lo[:, c0:c0 + w], block_ones, preferred_element_type=F32))
    ms = outs[0] if len(outs) == 1 else jnp.concatenate(outs, axis=1)
    return ms * (1.0 / SWA_HEAD_DIM)


def _norm_rope(x, nw, cos_f, sin_f, block_ones):
    width = x.shape[1]
    y = x * lax.rsqrt(_head_mean_sq(x, block_ones) + EPS) * nw
    lane = lax.broadcasted_iota(jnp.int32, y.shape, 1) % SWA_HEAD_DIM
    partner = jnp.where(lane < ROPE_HALF,
                        pltpu.roll(y, width - ROPE_HALF, axis=1),
                        pltpu.roll(y, ROPE_HALF, axis=1))
    return y * cos_f + partner * sin_f


def _swa_kernel(sink_ref, q_ref, k_ref, v_ref, cos_ref, sin_ref, qn_ref, kn_ref, o_ref,
                kprev, vprev):
    n = pl.program_id(1)
    blk = SWA_BLOCK

    @pl.when(n == 0)
    def _():
        kprev[...] = jnp.zeros_like(kprev)
        vprev[...] = jnp.zeros_like(vprev)

    r = lax.broadcasted_iota(jnp.int32, (2 * LANES, 2 * LANES), 0) // SWA_HEAD_DIM
    cc = lax.broadcasted_iota(jnp.int32, (2 * LANES, 2 * LANES), 1) // SWA_HEAD_DIM
    block_ones = jnp.where(r == cc, 1.0, 0.0).astype(BF16)

    cos_f = cos_ref[...]
    sin_f = sin_ref[...]
    q = _norm_rope(q_ref[...].astype(F32), qn_ref[...],
                   jnp.tile(cos_f, (1, SWA_WIDTH // LANES)), jnp.tile(sin_f, (1, SWA_WIDTH // LANES)),
                   block_ones) * (SWA_HEAD_DIM ** -0.5)
    k_cur = _norm_rope(k_ref[...].astype(F32), kn_ref[...], cos_f, sin_f,
                       block_ones[:LANES, :LANES]).astype(BF16)
    v_cur = v_ref[...]

    k_band = jnp.concatenate([kprev[...], k_cur], axis=0)
    v_band = jnp.concatenate([vprev[...], v_cur], axis=0)
    kprev[...] = k_cur
    vprev[...] = v_cur

    low = lax.broadcasted_iota(jnp.int32, (2 * blk, LANES), 1) < SWA_HEAD_DIM

    def split_heads(band):
        bf = band.astype(F32)
        swap = pltpu.roll(bf, SWA_HEAD_DIM, axis=1)
        even = (jnp.where(low, bf, 0.0).astype(BF16), jnp.where(low, swap, 0.0).astype(BF16))
        odd = (jnp.where(low, 0.0, swap).astype(BF16), jnp.where(low, 0.0, bf).astype(BF16))
        return even, odd

    k_even, k_odd = split_heads(k_band)
    v_even, v_odd = split_heads(v_band)

    qi = lax.broadcasted_iota(jnp.int32, (blk, 2 * blk), 0)
    kj = lax.broadcasted_iota(jnp.int32, (blk, 2 * blk), 1)
    rel = qi + blk - kj
    bias = jnp.where(rel >= 0, jnp.where(rel < blk, 0.0, NEG_BIG), NEG_BIG)
    prev_penalty = jnp.where(n > 0, 0.0, NEG_BIG)
    bias = bias + jnp.where(kj < blk, prev_penalty, 0.0)

    qb = q.astype(BF16)
    group = SWA_Q_HEADS // SWA_KV_HEADS
    for pair in range(SWA_Q_HEADS // 2):
        kv = (2 * pair) // group
        q2 = qb[:, pair * LANES:(pair + 1) * LANES]
        acc = None
        for half, (kop, vop) in enumerate(((k_even[kv], v_even[kv]), (k_odd[kv], v_odd[kv]))):
            sink = sink_ref[2 * pair + half]
            s = _mm_nt(q2, kop) + bias
            mx = jnp.maximum(jnp.max(s, axis=-1, keepdims=True), sink)
            p = jnp.exp(s - mx)
            denom = jnp.sum(p, axis=-1, keepdims=True) + jnp.exp(sink - mx)
            contrib = _mm(p, vop) * (1.0 / denom)
            acc = contrib if acc is None else acc + contrib
        o_ref[:, pair * LANES:(pair + 1) * LANES] = acc.astype(BF16)


def _swa(proj, cos_t, sin_t, sinks, qn, kn, batch, seq):
    m = proj.shape[0]
    blk = SWA_BLOCK
    nb = seq // blk
    row = lambda b, n: b * nb + n
    return pl.pallas_call(
        _swa_kernel,
        grid=(batch, nb),
        in_specs=[pl.BlockSpec(memory_space=pltpu.SMEM),
                  pl.BlockSpec((blk, SWA_WIDTH), lambda b, n: (row(b, n), COL_SWQ // SWA_WIDTH)),
                  pl.BlockSpec((blk, SWA_KV_WIDTH), lambda b, n: (row(b, n), COL_SWK // SWA_KV_WIDTH)),
                  pl.BlockSpec((blk, SWA_KV_WIDTH), lambda b, n: (row(b, n), COL_SWV // SWA_KV_WIDTH)),
                  pl.BlockSpec((blk, LANES), lambda b, n: (row(b, n), 0)),
                  pl.BlockSpec((blk, LANES), lambda b, n: (row(b, n), 0)),
                  pl.BlockSpec((1, SWA_WIDTH), lambda b, n: (0, 0)),
                  pl.BlockSpec((1, SWA_KV_WIDTH), lambda b, n: (0, 0))],
        out_specs=pl.BlockSpec((blk, SWA_WIDTH), lambda b, n: (row(b, n), 0)),
        out_shape=jax.ShapeDtypeStruct((m, SWA_WIDTH), BF16),
        scratch_shapes=[pltpu.VMEM((blk, SWA_KV_WIDTH), BF16),
                        pltpu.VMEM((blk, SWA_KV_WIDTH), BF16)],
        compiler_params=_params("arbitrary", "arbitrary"),
        name="swa",
    )(sinks, proj, proj, proj, cos_t, sin_t, qn, kn)


def _merge_kernel(x_ref, odn_ref, osw_ref, ga_ref, gb_ref, gt_ref, wdn_ref, wsw_ref, wo_ref, o_ref):
    ya = jnp.dot(odn_ref[...], wdn_ref[...], preferred_element_type=F32)
    yb = jnp.dot(osw_ref[...], wsw_ref[...], preferred_element_type=F32)
    merged = _sigmoid(ga_ref[...].astype(F32)) * ya + _sigmoid(gb_ref[...].astype(F32)) * yb
    out = jnp.dot(merged.astype(BF16), wo_ref[...], preferred_element_type=F32)
    o_ref[...] = x_ref[...] + gt_ref[0] * out


def _merge(x, o_dn, o_sw, proj, mod, w_dn, w_sw, w_o, seq):
    m = x.shape[0]
    tm = 512
    tiles_per_seq = seq // tm
    tok = lambda cb: pl.BlockSpec((tm, D_MODEL), lambda i: (i, cb))
    wfull = pl.BlockSpec((D_MODEL, D_MODEL), lambda i: (0, 0))
    return pl.pallas_call(
        _merge_kernel,
        grid=(m // tm,),
        in_specs=[tok(0), tok(0), tok(0), tok(COL_GA // D_MODEL), tok(COL_GB // D_MODEL),
                  pl.BlockSpec((1, 1, D_MODEL), lambda i: ((i // tiles_per_seq) * 6 + 2, 0, 0)),
                  wfull, wfull, wfull],
        out_specs=tok(0),
        out_shape=jax.ShapeDtypeStruct((m, D_MODEL), F32),
        input_output_aliases={0: 0},
        compiler_params=_params("arbitrary"),
        name="merge_out",
    )(x, o_dn, o_sw, proj, proj, mod, w_dn, w_sw, w_o)


def _ffn_kernel(x_ref, nw_ref, sc_ref, sh_ref, gt_ref, wa_ref, wl_ref, cw_ref, cb_ref, wd_ref,
                o_ref, h_scr, acc_scr, abuf, halo_scr, *, tm, tiles_per_seq, n_ff_tiles):
    i = pl.program_id(0)
    j = pl.program_id(1)
    halo = SUBLANES

    @pl.when(j == 0)
    def _():
        h_scr[...] = _norm_mod(x_ref[...], nw_ref[...], sc_ref[0], sh_ref[0]).astype(BF16)

    h = h_scr[...]
    a = jnp.dot(h, wa_ref[...], preferred_element_type=F32)
    lin = jnp.dot(h, wl_ref[...], preferred_element_type=F32)

    first = (i % tiles_per_seq) == 0
    prev = halo_scr[j]
    abuf[0:halo, :] = jnp.where(first, jnp.zeros_like(prev), prev)
    abuf[halo:halo + tm, :] = a
    halo_scr[j] = a[tm - halo:tm, :]
    w = cw_ref[...]
    y = w[FFN_CONV - 1:FFN_CONV, :] * a + cb_ref[...]
    for s in range(1, FFN_CONV):
        y = y + w[FFN_CONV - 1 - s:FFN_CONV - s, :] * abuf[halo - s:halo - s + tm, :]
    act = (_silu(y) * lin).astype(BF16)
    part = jnp.dot(act, wd_ref[...], preferred_element_type=F32)

    @pl.when(j == 0)
    def _():
        acc_scr[...] = part

    @pl.when(j > 0)
    def _():
        acc_scr[...] += part

    @pl.when(j == n_ff_tiles - 1)
    def _():
        o_ref[...] = x_ref[...] + gt_ref[0] * acc_scr[...]


def _ffn(x, nw, mod, w_up, conv_w, conv_b, w_down, seq):
    m = x.shape[0]
    tm = 512
    n_ff = 2
    fc = D_FF // n_ff
    tiles_per_seq = seq // tm
    kern = functools.partial(_ffn_kernel, tm=tm, tiles_per_seq=tiles_per_seq, n_ff_tiles=n_ff)
    modspec = lambda k: pl.BlockSpec((1, 1, D_MODEL), lambda i, j: ((i // tiles_per_seq) * 6 + k, 0, 0))
    return pl.pallas_call(
        kern,
        grid=(m // tm, n_ff),
        in_specs=[pl.BlockSpec((tm, D_MODEL), lambda i, j: (i, 0)),
                  pl.BlockSpec((1, D_MODEL), lambda i, j: (0, 0)),
                  modspec(4), modspec(3), modspec(5),
                  pl.BlockSpec((D_MODEL, fc), lambda i, j: (0, j)),
                  pl.BlockSpec((D_MODEL, fc), lambda i, j: (0, n_ff + j)),
                  pl.BlockSpec((FFN_CONV, fc), lambda i, j: (0, j)),
                  pl.BlockSpec((1, fc), lambda i, j: (0, j)),
                  pl.BlockSpec((fc, D_MODEL), lambda i, j: (j, 0))],
        out_specs=pl.BlockSpec((tm, D_MODEL), lambda i, j: (i, 0)),
        out_shape=jax.ShapeDtypeStruct((m, D_MODEL), F32),
        scratch_shapes=[pltpu.VMEM((tm, D_MODEL), BF16),
                        pltpu.VMEM((tm, D_MODEL), F32),
                        pltpu.VMEM((tm + SUBLANES, fc), F32),
                        pltpu.VMEM((n_ff, SUBLANES, fc), F32)],
        input_output_aliases={0: 0},
        compiler_params=_params("arbitrary", "arbitrary"),
        name="ffn",
    )(x, nw, mod, mod, mod, w_up, w_up, conv_w, conv_b, w_down)


def _pack_w_in(w_in):
    depth = w_in.shape[0]
    o_z, o_a = 3 * DN_WIDTH, 4 * DN_WIDTH
    o_swq = o_a + 2 * DN_HEADS
    o_swk = o_swq + SWA_WIDTH
    o_swv = o_swk + SWA_KV_WIDTH
    o_ga = o_swv + SWA_KV_WIDTH
    o_gb = o_ga + D_MODEL
    del o_z
    pad = jnp.zeros((depth, D_MODEL, IN_PACKED - COL_AB - 2 * DN_HEADS), w_in.dtype)
    packed = jnp.concatenate(
        [w_in[:, :, :o_a], w_in[:, :, o_swq:o_swk], w_in[:, :, o_ga:o_gb], w_in[:, :, o_gb:],
         w_in[:, :, o_swk:o_swv], w_in[:, :, o_swv:o_ga], w_in[:, :, o_a:o_swq], pad], axis=2)
    return packed.astype(BF16)


def _lane_row(v):
    depth, n = v.shape
    return jnp.zeros((depth, 1, LANES), F32).at[:, 0, :n].set(v.astype(F32))


def kernel(x, c, positions, w_ada, b_ada, norm_mix, w_in, dn_conv, dn_a_log, dn_dt_bias, dn_norm,
           w_dn_out, swa_q_norm, swa_k_norm, swa_sinks, w_swa_out, w_o, norm_ffn, w_up, ffn_conv,
           ffn_conv_b, w_down):
    batch, seq, _ = x.shape
    depth = w_ada.shape[0]
    m = batch * seq

    mod_all = _ada_mod(c, w_ada, b_ada)
    cos_t, sin_t = _rope_tables(positions)

    w_in_p = _pack_w_in(w_in)
    w_dn_b, w_sw_b, w_o_b = w_dn_out.astype(BF16), w_swa_out.astype(BF16), w_o.astype(BF16)
    w_up_b, w_down_b = w_up.astype(BF16), w_down.astype(BF16)
    alog = _lane_row(dn_a_log)
    dtb = _lane_row(dn_dt_bias)
    qn = jnp.tile(swa_q_norm, (1, SWA_Q_HEADS)).reshape(depth, 1, SWA_WIDTH)
    kn = jnp.tile(swa_k_norm, (1, SWA_KV_HEADS)).reshape(depth, 1, SWA_KV_WIDTH)

    xf = x.reshape(m, D_MODEL)
    for l in range(depth):
        mod = mod_all[l].reshape(SUBLANES * 6, 1, D_MODEL)
        proj, gates = _inproj(xf, norm_mix[l].reshape(1, D_MODEL), mod, w_in_p[l], alog[l], dtb[l], seq)
        o_dn = _deltanet(proj, gates, dn_conv[l], dn_norm[l].reshape(1, DN_HEAD_DIM), batch, seq)
        o_sw = _swa(proj, cos_t, sin_t, swa_sinks[l], qn[l], kn[l], batch, seq)
        xf = _merge(xf, o_dn, o_sw, proj, mod, w_dn_b[l], w_sw_b[l], w_o_b[l], seq)
        xf = _ffn(xf, norm_ffn[l].reshape(1, D_MODEL), mod, w_up_b[l], ffn_conv[l],
                  ffn_conv_b[l].reshape(1, D_FF), w_down_b[l], seq)
    return xf.reshape(batch, seq, D_MODEL)
```

```python
import functools

import numpy as np
import jax
import jax.numpy as jnp
from jax import lax
from jax.experimental import pallas as pl
from jax.experimental.pallas import tpu as pltpu

F32 = jnp.float32
BF16 = jnp.bfloat16

D_MODEL = 1024
DN_HEADS = 8
DN_HEAD_DIM = 128
DN_WIDTH = DN_HEADS * DN_HEAD_DIM
DN_CONV = 4
DN_CHUNK = 64
SWA_Q_HEADS = 16
SWA_KV_HEADS = 2
SWA_HEAD_DIM = 64
SWA_WIDTH = SWA_Q_HEADS * SWA_HEAD_DIM
SWA_KV_WIDTH = SWA_KV_HEADS * SWA_HEAD_DIM
SWA_BLOCK = 128
SWA_HEAD_BATCH = 4
ROPE_THETA = 500000.0
ROPE_DIM = SWA_HEAD_DIM // 4
ROPE_HALF = ROPE_DIM // 2
D_FF = 2816
FFN_CONV = 3
EPS = 1e-6

LANES = 128
SUBLANES = 8
VMEM_LIMIT = 56 * 1024 * 1024

COL_Q, COL_K, COL_V, COL_Z = 0, 1024, 2048, 3072
COL_SWQ, COL_GA, COL_GB = 4096, 5120, 6144
COL_SWK, COL_SWV, COL_AB = 7168, 7296, 7424
IN_PACKED = 7680
NEG_BIG = -1e30


def _sigmoid(x):
    return 1.0 / (1.0 + jnp.exp(-x))


def _silu(x):
    return x * _sigmoid(x)


def _mm(a, b):
    return jnp.dot(a.astype(BF16), b.astype(BF16), preferred_element_type=F32)


def _mm_nt(a, b):
    return lax.dot_general(a.astype(BF16), b.astype(BF16), (((1,), (1,)), ((), ())),
                           preferred_element_type=F32)


def _params(*sem):
    return pltpu.CompilerParams(dimension_semantics=sem, vmem_limit_bytes=VMEM_LIMIT)


def _ada_kernel(c_ref, w_ref, b_ref, o_ref):
    ca = _silu(c_ref[...])
    o_ref[0] = jnp.dot(ca, w_ref[0], precision=lax.Precision.HIGHEST,
                       preferred_element_type=F32) + b_ref[0]


def _ada_mod(c, w_ada, b_ada):
    depth = w_ada.shape[0]
    batch = c.shape[0]
    n_out = w_ada.shape[2]
    tn = 1536
    c_pad = jnp.zeros((SUBLANES, D_MODEL), F32).at[:batch].set(c)
    return pl.pallas_call(
        _ada_kernel,
        grid=(depth, n_out // tn),
        in_specs=[pl.BlockSpec((SUBLANES, D_MODEL), lambda l, j: (0, 0)),
                  pl.BlockSpec((1, D_MODEL, tn), lambda l, j: (l, 0, j)),
                  pl.BlockSpec((1, 1, tn), lambda l, j: (l, 0, j))],
        out_specs=pl.BlockSpec((1, SUBLANES, tn), lambda l, j: (l, 0, j)),
        out_shape=jax.ShapeDtypeStruct((depth, SUBLANES, n_out), F32),
        compiler_params=_params("arbitrary", "arbitrary"),
        name="ada_mod",
    )(c_pad, w_ada, b_ada.reshape(depth, 1, n_out))


def _rope_kernel(pos_ref, inv_ref, sgn_ref, cos_ref, sin_ref):
    ang = pos_ref[...] * inv_ref[...]
    on = sgn_ref[...] != 0.0
    cos_ref[...] = jnp.where(on, jnp.cos(ang), 1.0)
    sin_ref[...] = jnp.sin(ang) * sgn_ref[...]


def _rope_tables(positions):
    m = positions.size
    tm = min(2048, m)
    lane = np.arange(LANES) % SWA_HEAD_DIM
    inv = np.where(lane < ROPE_DIM,
                   np.power(ROPE_THETA, -(lane % ROPE_HALF).astype(np.float64) / ROPE_HALF), 0.0)
    sgn = np.where(lane < ROPE_HALF, -1.0, np.where(lane < ROPE_DIM, 1.0, 0.0))
    pos = positions.astype(F32).reshape(m, 1)
    return pl.pallas_call(
        _rope_kernel,
        grid=(m // tm,),
        in_specs=[pl.BlockSpec((tm, 1), lambda i: (i, 0)),
                  pl.BlockSpec((1, LANES), lambda i: (0, 0)),
                  pl.BlockSpec((1, LANES), lambda i: (0, 0))],
        out_specs=[pl.BlockSpec((tm, LANES), lambda i: (i, 0)),
                   pl.BlockSpec((tm, LANES), lambda i: (i, 0))],
        out_shape=[jax.ShapeDtypeStruct((m, LANES), F32)] * 2,
        compiler_params=_params("arbitrary"),
        name="rope_tables",
    )(pos, jnp.asarray(inv, F32).reshape(1, LANES), jnp.asarray(sgn, F32).reshape(1, LANES))


def _norm_mod(x, nw, sc, sh):
    ms = jnp.mean(x * x, axis=-1, keepdims=True)
    return (x * lax.rsqrt(ms + EPS) * nw) * (1.0 + sc) + sh


def _inproj_kernel(x_ref, nw_ref, sc_ref, sh_ref, w_ref, alog_ref, dtb_ref,
                   proj_ref, gate_ref, h_scr, *, n_col_tiles, ab_off):
    j = pl.program_id(1)

    @pl.when(j == 0)
    def _():
        h_scr[...] = _norm_mod(x_ref[...], nw_ref[...], sc_ref[0], sh_ref[0]).astype(BF16)

    acc = jnp.dot(h_scr[...], w_ref[...], preferred_element_type=F32)
    proj_ref[...] = acc.astype(BF16)

    @pl.when(j == n_col_tiles - 1)
    def _():
        ab = acc[:, ab_off:ab_off + LANES]
        z = ab + dtb_ref[...]
        softplus = jnp.maximum(z, 0.0) + jnp.log(1.0 + jnp.exp(-jnp.abs(z)))
        g = -jnp.exp(alog_ref[...]) * softplus
        lane = lax.broadcasted_iota(jnp.int32, ab.shape, 1)
        gate_ref[...] = jnp.where(lane < DN_HEADS, g, _sigmoid(ab))


def _mod_spec(layer, k, tiles_per_seq):
    return pl.BlockSpec((1, 1, D_MODEL),
                        lambda i, *_: ((layer * SUBLANES + i // tiles_per_seq) * 6 + k, 0, 0))


def _inproj(x, nw, mod, w, alog, dtb, layer, seq):
    m = x.shape[0]
    tm, tn = 1024, 1536
    nj = IN_PACKED // tn
    tiles_per_seq = seq // tm
    kern = functools.partial(_inproj_kernel, n_col_tiles=nj, ab_off=COL_AB - (nj - 1) * tn)
    return pl.pallas_call(
        kern,
        grid=(m // tm, nj),
        in_specs=[pl.BlockSpec((tm, D_MODEL), lambda i, j: (i, 0)),
                  pl.BlockSpec((None, 1, D_MODEL), lambda i, j: (layer, 0, 0)),
                  _mod_spec(layer, 1, tiles_per_seq), _mod_spec(layer, 0, tiles_per_seq),
                  pl.BlockSpec((None, D_MODEL, tn), lambda i, j: (layer, 0, j)),
                  pl.BlockSpec((None, 1, LANES), lambda i, j: (layer, 0, 0)),
                  pl.BlockSpec((None, 1, LANES), lambda i, j: (layer, 0, 0))],
        out_specs=[pl.BlockSpec((tm, tn), lambda i, j: (i, j)),
                   pl.BlockSpec((tm, LANES), lambda i, j: (i, 0))],
        out_shape=[jax.ShapeDtypeStruct((m, IN_PACKED), BF16),
                   jax.ShapeDtypeStruct((m, LANES), F32)],
        scratch_shapes=[pltpu.VMEM((tm, D_MODEL), BF16)],
        compiler_params=_params("arbitrary", "arbitrary"),
        name="inproj",
    )(x, nw, mod, mod, w, alog, dtb)


DN_BLOCK = 256
INV_BLOCK = 16
DN_PREV_ROWS = 16
DN_HEAD_GROUP = 4
M_INCL, M_STRICT, M_DIAG, M_EYE, M_OFF0 = 0, 1, 2, 3, 4


def _dn_masks():
    r = np.arange(DN_BLOCK)[:, None]
    c = np.arange(DN_BLOCK)[None, :]
    same = lambda b: (r // b) == (c // b)
    chunk = same(DN_CHUNK)
    masks = [chunk & (r >= c), chunk & (r > c), same(INV_BLOCK), r == c]
    b = INV_BLOCK
    while b < DN_CHUNK:
        masks.append(same(2 * b) & ~same(b))
        b *= 2
    return np.stack(masks).astype(np.float32)


def _inv_unit_lower(l_strict, mask_ref, filler):
    dot = functools.partial(jnp.dot, preferred_element_type=F32)
    ds = [l * mask_ref[M_DIAG] for l in l_strict]
    ts = [mask_ref[M_EYE] - d for d in ds]
    dbs = [d.astype(BF16) for d in ds]
    mpows = [dot(db, db) for db in dbs]
    filler()
    n_fac = int(np.log2(INV_BLOCK)) - 1
    for i in range(n_fac):
        mbs = [m.astype(BF16) for m in mpows]
        ts = [t + dot(t.astype(BF16), mb) for t, mb in zip(ts, mbs)]
        if i < n_fac - 1:
            mpows = [dot(mb, mb) for mb in mbs]
        filler()
    n_levels = int(np.log2(DN_CHUNK // INV_BLOCK))
    for lvl in range(n_levels):
        tbs = [t.astype(BF16) for t in ts]
        inner = [dot((l * mask_ref[M_OFF0 + lvl]).astype(BF16), tb).astype(BF16)
                 for l, tb in zip(l_strict, tbs)]
        filler()
        ts = [t - dot(tb, inn) for t, tb, inn in zip(ts, tbs, inner)]
        filler()
    return ts


def _dn_conv_pieces(raw_refs, prev_refs, keep_prev, cw_ref, xbuf, act_ref):
    tb, halo = DN_BLOCK, DN_PREV_ROWS

    def piece(idx, h):
        sl = slice(h * DN_HEAD_DIM, (h + 1) * DN_HEAD_DIM)
        ref, pref = raw_refs[idx], prev_refs[idx]
        if pref is None:
            xbuf[idx, 0:halo, sl] = jnp.zeros((halo, DN_HEAD_DIM), F32)
        else:
            xbuf[idx, 0:halo, sl] = pref[:, sl].astype(F32) * keep_prev
        x = ref[:, sl].astype(F32)
        xbuf[idx, halo:halo + tb, sl] = x
        w = cw_ref[:, idx * DN_WIDTH + h * DN_HEAD_DIM:idx * DN_WIDTH + (h + 1) * DN_HEAD_DIM]
        y = w[DN_CONV - 1:DN_CONV, :] * x
        for s in range(1, DN_CONV):
            y = y + w[DN_CONV - 1 - s:DN_CONV - s, :] * xbuf[idx, halo - s:halo - s + tb, sl]
        y = _silu(y)
        if idx < 2:
            scale = DN_HEAD_DIM ** -0.5 if idx == 0 else 1.0
            y = y * (lax.rsqrt(jnp.sum(y * y, axis=-1, keepdims=True) + EPS) * scale)
        act_ref[idx, :, sl] = y

    return [functools.partial(piece, idx, h) for idx in range(3) for h in range(DN_HEADS)]


def _dn_intra_kernel(q_ref, k_ref, v_ref, qp_ref, kp_ref, vp_ref, gate_ref, cw_ref, mask_ref,
                     u_ref, w_ref, qd_ref, kd_ref, qk_ref, egl_ref, xbuf, act_cur, *, blocks_per_seq):
    i = pl.program_id(0)
    tb, c = DN_BLOCK, DN_CHUNK
    n_chunks = tb // c

    keep_prev = jnp.where((i % blocks_per_seq) == 0, 0.0, 1.0)
    pieces = _dn_conv_pieces((q_ref, k_ref, v_ref), (qp_ref, kp_ref, vp_ref), keep_prev,
                             cw_ref, xbuf, act_cur)
    head_of = [h for _ in range(3) for h in range(DN_HEADS)]
    pending = [p for p, h in zip(pieces, head_of) if h >= DN_HEAD_GROUP]
    for p, h in zip(pieces, head_of):
        if h < DN_HEAD_GROUP:
            p()

    gates = gate_ref[...]
    gcum = jnp.dot(mask_ref[M_INCL], gates, precision=lax.Precision.HIGHEST,
                   preferred_element_type=F32)
    gcum_t = gcum.T
    glast = jnp.concatenate(
        [jnp.broadcast_to(gcum[ci * c + c - 1:ci * c + c, :], (c, LANES)) for ci in range(n_chunks)], axis=0)
    e_cum = jnp.exp(gcum)
    e_rem = jnp.exp(glast - gcum)
    for ci in range(n_chunks):
        gl = gcum_t[0:DN_HEADS, ci * c + c - 1:ci * c + c]
        egl_ref[ci * DN_HEADS:(ci + 1) * DN_HEADS, :] = jnp.broadcast_to(jnp.exp(gl), (DN_HEADS, LANES))

    def filler():
        if pending:
            pending.pop(0)()

    for h0 in range(0, DN_HEADS, DN_HEAD_GROUP):
        heads = range(h0, h0 + DN_HEAD_GROUP)
        sls = [slice(h * DN_HEAD_DIM, (h + 1) * DN_HEAD_DIM) for h in heads]
        rhs, ls = [], []
        if h0 > 0:
            while pending:
                filler()
        for h, sl in zip(heads, sls):
            qh, kh, vh = act_cur[0, :, sl], act_cur[1, :, sl], act_cur[2, :, sl]
            beta = jnp.broadcast_to(gates[:, DN_HEADS + h:DN_HEADS + h + 1], kh.shape)
            eg = jnp.broadcast_to(e_cum[:, h:h + 1], kh.shape)
            kb = kh * beta
            qd_ref[:, sl] = (qh * eg).astype(BF16)
            kd_ref[:, sl] = (kh * e_rem[:, h:h + 1]).astype(BF16)
            rhs.append(jnp.concatenate([vh * beta, kb * eg], axis=1).astype(BF16))
            filler()
            a = _mm_nt(jnp.concatenate([kb, qh], axis=0), kh)
            decay = jnp.exp(jnp.minimum(gcum[:, h:h + 1] - gcum_t[h:h + 1, :], 0.0)) * mask_ref[M_INCL]
            ls.append(a[:tb] * (decay * mask_ref[M_STRICT]))
            qk = a[tb:] * decay
            qk_ref[:, sl] = jnp.concatenate(
                [qk[ci * c:(ci + 1) * c, (ci // 2) * LANES:(ci // 2 + 1) * LANES] for ci in range(n_chunks)],
                axis=0).astype(BF16)
        tinvs = _inv_unit_lower(ls, mask_ref, filler)
        for sl, tinv, r in zip(sls, tinvs, rhs):
            uw = jnp.dot(tinv.astype(BF16), r, preferred_element_type=F32)
            u_ref[:, sl] = uw[:, :DN_HEAD_DIM]
            w_ref[:, sl] = uw[:, DN_HEAD_DIM:].astype(BF16)


def _dn_scan_kernel(u_ref, w_ref, qd_ref, kd_ref, qk_ref, z_ref, egl_ref, nw_ref, o_ref, s_scr, *, tb):
    t = pl.program_id(1)
    c = DN_CHUNK

    @pl.when(t == 0)
    def _():
        s_scr[...] = jnp.zeros_like(s_scr)

    nw = nw_ref[...]
    zeros_v = jnp.zeros((c, DN_HEAD_DIM), BF16)

    def pair_body(pi, carry):
        for par in range(2):
            ci = 2 * pi + par
            r0 = pl.multiple_of(ci * c, c)
            heads = range(DN_HEADS)
            sls = [slice(h * DN_HEAD_DIM, (h + 1) * DN_HEAD_DIM) for h in heads]
            rows = pl.ds(r0, c)
            s_old = [s_scr[h] for h in heads]
            ws = [jnp.dot(jnp.concatenate([w_ref[rows, sl], qd_ref[rows, sl]], axis=0),
                          s.astype(BF16), preferred_element_type=F32) for sl, s in zip(sls, s_old)]
            vbs = [(u_ref[rows, sl] - w_s[:c]).astype(BF16) for sl, w_s in zip(sls, ws)]
            kd_ts = [kd_ref[rows, sl].astype(F32).T.astype(BF16) for sl in sls]
            for h, sl, s, vb, kd_t in zip(heads, sls, s_old, vbs, kd_ts):
                egl = egl_ref[pl.ds(ci * DN_HEADS + h, 1), :]
                s_scr[h] = s * egl + jnp.dot(kd_t, vb, preferred_element_type=F32)
            for sl, w_s, vb in zip(sls, ws, vbs):
                v_pad = jnp.concatenate([vb, zeros_v] if par == 0 else [zeros_v, vb], axis=0)
                o = w_s[c:] + jnp.dot(qk_ref[rows, sl], v_pad, preferred_element_type=F32)
                o = o * lax.rsqrt(jnp.mean(o * o, axis=-1, keepdims=True) + EPS) * nw
                zh = z_ref[rows, sl].astype(F32)
                o_ref[rows, sl] = (o * _silu(zh)).astype(BF16)
        return carry

    lax.fori_loop(0, tb // (2 * c), pair_body, 0)


def _deltanet(proj, gates, conv_w, norm_w, layer, batch, seq):
    m = proj.shape[0]
    tb = DN_BLOCK
    nblk = m // tb
    masks = jnp.asarray(_dn_masks())
    prev_per_blk = tb // DN_PREV_ROWS
    cols = (COL_Q // DN_WIDTH, COL_K // DN_WIDTH, COL_V // DN_WIDTH)
    cur = [pl.BlockSpec((tb, DN_WIDTH), lambda i, cb=cb: (i, cb)) for cb in cols]
    prev = [pl.BlockSpec((DN_PREV_ROWS, DN_WIDTH),
                         lambda i, cb=cb: (jnp.maximum(i * prev_per_blk - 1, 0), cb)) for cb in cols]
    tok = pl.BlockSpec((tb, DN_WIDTH), lambda i: (i, 0))
    egl_rows = (tb // DN_CHUNK) * DN_HEADS
    bf_out = jax.ShapeDtypeStruct((m, DN_WIDTH), BF16)
    u, w, qd, kd, qk, egl = pl.pallas_call(
        functools.partial(_dn_intra_kernel, blocks_per_seq=seq // tb),
        grid=(nblk,),
        in_specs=cur + prev + [
            pl.BlockSpec((tb, LANES), lambda i: (i, 0)),
            pl.BlockSpec((None, DN_CONV, 3 * DN_WIDTH), lambda i: (layer, 0, 0)),
            pl.BlockSpec(masks.shape, lambda i: (0, 0, 0))],
        out_specs=[tok, tok, tok, tok, tok, pl.BlockSpec((egl_rows, LANES), lambda i: (i, 0))],
        out_shape=[jax.ShapeDtypeStruct((m, DN_WIDTH), F32), bf_out, bf_out, bf_out, bf_out,
                   jax.ShapeDtypeStruct((m // DN_CHUNK * DN_HEADS, LANES), F32)],
        scratch_shapes=[pltpu.VMEM((3, tb + DN_PREV_ROWS, DN_WIDTH), F32),
                        pltpu.VMEM((3, tb, DN_WIDTH), F32)],
        compiler_params=_params("parallel"),
        name="dn_intra",
    )(*([proj] * 6), gates, conv_w, masks)

    ts = 256
    nt = seq // ts
    blk = lambda cb: pl.BlockSpec((ts, DN_WIDTH), lambda b, t: (b * nt + t, cb))
    return pl.pallas_call(
        functools.partial(_dn_scan_kernel, tb=ts),
        grid=(batch, nt),
        in_specs=[blk(0), blk(0), blk(0), blk(0), blk(0), blk(COL_Z // DN_WIDTH),
                  pl.BlockSpec((ts // DN_CHUNK * DN_HEADS, LANES), lambda b, t: (b * nt + t, 0)),
                  pl.BlockSpec((None, 1, DN_HEAD_DIM), lambda b, t: (layer, 0, 0))],
        out_specs=blk(0),
        out_shape=bf_out,
        scratch_shapes=[pltpu.VMEM((DN_HEADS, DN_HEAD_DIM, DN_HEAD_DIM), F32)],
        compiler_params=_params("arbitrary", "arbitrary"),
        name="dn_scan",
    )(u, w, qd, kd, qk, proj, egl, norm_w)


def _head_mean_sq(x, block_ones):
    x2 = x * x
    hi = x2.astype(BF16)
    lo = (x2 - hi.astype(F32)).astype(BF16)
    outs = []
    w = block_ones.shape[0]
    for c0 in range(0, x.shape[1], w):
        outs.append(jnp.dot(hi[:, c0:c0 + w], block_ones, preferred_element_type=F32)
                    + jnp.dot(lo[:, c0:c0 + w], block_ones, preferred_element_type=F32))
    ms = outs[0] if len(outs) == 1 else jnp.concatenate(outs, axis=1)
    return ms * (1.0 / SWA_HEAD_DIM)


def _norm_rope(x, nw, cos_f, sin_f, block_ones):
    width = x.shape[1]
    y = x * lax.rsqrt(_head_mean_sq(x, block_ones) + EPS) * nw
    lane = lax.broadcasted_iota(jnp.int32, y.shape, 1) % SWA_HEAD_DIM
    partner = jnp.where(lane < ROPE_HALF,
                        pltpu.roll(y, width - ROPE_HALF, axis=1),
                        pltpu.roll(y, ROPE_HALF, axis=1))
    return y * cos_f + partner * sin_f


def _swa_kernel(sink_ref, q_ref, k_ref, v_ref, cos_ref, sin_ref, qn_ref, kn_ref, o_ref,
                kprev, vprev):
    n = pl.program_id(1)
    blk = SWA_BLOCK

    @pl.when(n == 0)
    def _():
        kprev[...] = jnp.zeros_like(kprev)
        vprev[...] = jnp.zeros_like(vprev)

    r = lax.broadcasted_iota(jnp.int32, (2 * LANES, 2 * LANES), 0) // SWA_HEAD_DIM
    cc = lax.broadcasted_iota(jnp.int32, (2 * LANES, 2 * LANES), 1) // SWA_HEAD_DIM
    block_ones = jnp.where(r == cc, 1.0, 0.0).astype(BF16)

    cos_f = cos_ref[...]
    sin_f = sin_ref[...]
    q = _norm_rope(q_ref[...].astype(F32), qn_ref[...],
                   jnp.tile(cos_f, (1, SWA_WIDTH // LANES)), jnp.tile(sin_f, (1, SWA_WIDTH // LANES)),
                   block_ones) * (SWA_HEAD_DIM ** -0.5)
    k_cur = _norm_rope(k_ref[...].astype(F32), kn_ref[...], cos_f, sin_f,
                       block_ones[:LANES, :LANES]).astype(BF16)
    v_cur = v_ref[...]

    k_band = jnp.concatenate([kprev[...], k_cur], axis=0)
    v_band = jnp.concatenate([vprev[...], v_cur], axis=0)
    kprev[...] = k_cur
    vprev[...] = v_cur

    low = lax.broadcasted_iota(jnp.int32, (2 * blk, LANES), 1) < SWA_HEAD_DIM

    def split_heads(band):
        bf = band.astype(F32)
        swap = pltpu.roll(bf, SWA_HEAD_DIM, axis=1)
        even = (jnp.where(low, bf, 0.0).astype(BF16), jnp.where(low, swap, 0.0).astype(BF16))
        odd = (jnp.where(low, 0.0, swap).astype(BF16), jnp.where(low, 0.0, bf).astype(BF16))
        return even, odd

    k_even, k_odd = split_heads(k_band)
    v_even, v_odd = split_heads(v_band)

    qi = lax.broadcasted_iota(jnp.int32, (blk, 2 * blk), 0)
    kj = lax.broadcasted_iota(jnp.int32, (blk, 2 * blk), 1)
    rel = qi + blk - kj
    bias = jnp.where(rel >= 0, jnp.where(rel < blk, 0.0, NEG_BIG), NEG_BIG)
    prev_penalty = jnp.where(n > 0, 0.0, NEG_BIG)
    bias = bias + jnp.where(kj < blk, prev_penalty, 0.0)

    qb = q.astype(BF16)
    group = SWA_Q_HEADS // SWA_KV_HEADS
    for h0 in range(0, SWA_Q_HEADS, SWA_HEAD_BATCH):
        heads = range(h0, h0 + SWA_HEAD_BATCH)
        kops = [(k_even, k_odd)[h % 2][h // group] for h in heads]
        vops = [(v_even, v_odd)[h % 2][h // group] for h in heads]
        sinks = [sink_ref[h] for h in heads]
        ss = [_mm_nt(qb[:, (h // 2) * LANES:(h // 2 + 1) * LANES], kop) + bias
              for h, kop in zip(heads, kops)]
        mxs = [jnp.maximum(jnp.max(s, axis=-1, keepdims=True), sink) for s, sink in zip(ss, sinks)]
        ps = [jnp.exp(s - mx) for s, mx in zip(ss, mxs)]
        denoms = [jnp.sum(p, axis=-1, keepdims=True) + jnp.exp(sink - mx)
                  for p, sink, mx in zip(ps, sinks, mxs)]
        outs = [_mm(p, vop) * (1.0 / d) for p, vop, d in zip(ps, vops, denoms)]
        for pair in range(h0 // 2, (h0 + SWA_HEAD_BATCH) // 2):
            o_ref[:, pair * LANES:(pair + 1) * LANES] = (
                outs[2 * pair - h0] + outs[2 * pair + 1 - h0]).astype(BF16)


def _swa(proj, cos_t, sin_t, sinks, qn, kn, layer, batch, seq):
    m = proj.shape[0]
    blk = SWA_BLOCK
    nb = seq // blk
    row = lambda b, n: b * nb + n
    return pl.pallas_call(
        _swa_kernel,
        grid=(batch, nb),
        in_specs=[pl.BlockSpec(memory_space=pltpu.SMEM),
                  pl.BlockSpec((blk, SWA_WIDTH), lambda b, n: (row(b, n), COL_SWQ // SWA_WIDTH)),
                  pl.BlockSpec((blk, SWA_KV_WIDTH), lambda b, n: (row(b, n), COL_SWK // SWA_KV_WIDTH)),
                  pl.BlockSpec((blk, SWA_KV_WIDTH), lambda b, n: (row(b, n), COL_SWV // SWA_KV_WIDTH)),
                  pl.BlockSpec((blk, LANES), lambda b, n: (row(b, n), 0)),
                  pl.BlockSpec((blk, LANES), lambda b, n: (row(b, n), 0)),
                  pl.BlockSpec((None, 1, SWA_WIDTH), lambda b, n: (layer, 0, 0)),
                  pl.BlockSpec((None, 1, SWA_KV_WIDTH), lambda b, n: (layer, 0, 0))],
        out_specs=pl.BlockSpec((blk, SWA_WIDTH), lambda b, n: (row(b, n), 0)),
        out_shape=jax.ShapeDtypeStruct((m, SWA_WIDTH), BF16),
        scratch_shapes=[pltpu.VMEM((blk, SWA_KV_WIDTH), BF16),
                        pltpu.VMEM((blk, SWA_KV_WIDTH), BF16)],
        compiler_params=_params("arbitrary", "arbitrary"),
        name="swa",
    )(sinks, proj, proj, proj, cos_t, sin_t, qn, kn)


def _merge_kernel(x_ref, odn_ref, osw_ref, ga_ref, gb_ref, gt_ref, wdn_ref, wsw_ref, wo_ref, o_ref):
    ya = jnp.dot(odn_ref[...], wdn_ref[...], preferred_element_type=F32)
    yb = jnp.dot(osw_ref[...], wsw_ref[...], preferred_element_type=F32)
    merged = _sigmoid(ga_ref[...].astype(F32)) * ya + _sigmoid(gb_ref[...].astype(F32)) * yb
    out = jnp.dot(merged.astype(BF16), wo_ref[...], preferred_element_type=F32)
    o_ref[...] = x_ref[...] + gt_ref[0] * out


def _merge(x, o_dn, o_sw, proj, mod, w_dn, w_sw, w_o, layer, seq):
    m = x.shape[0]
    tm = 512
    tiles_per_seq = seq // tm
    tok = lambda cb: pl.BlockSpec((tm, D_MODEL), lambda i: (i, cb))
    wfull = pl.BlockSpec((None, D_MODEL, D_MODEL), lambda i: (layer, 0, 0))
    return pl.pallas_call(
        _merge_kernel,
        grid=(m // tm,),
        in_specs=[tok(0), tok(0), tok(0), tok(COL_GA // D_MODEL), tok(COL_GB // D_MODEL),
                  _mod_spec(layer, 2, tiles_per_seq), wfull, wfull, wfull],
        out_specs=tok(0),
        out_shape=jax.ShapeDtypeStruct((m, D_MODEL), F32),
        compiler_params=_params("arbitrary"),
        name="merge_out",
    )(x, o_dn, o_sw, proj, proj, mod, w_dn, w_sw, w_o)


def _ffn_kernel(x_ref, nw_ref, sc_ref, sh_ref, gt_ref, wa_ref, wl_ref, cw_ref, cb_ref, wd_ref,
                o_ref, h_scr, acc_scr, abuf, halo_scr, *, tm, tiles_per_seq, n_ff_tiles):
    i = pl.program_id(0)
    j = pl.program_id(1)
    halo = SUBLANES

    @pl.when(j == 0)
    def _():
        h_scr[...] = _norm_mod(x_ref[...], nw_ref[...], sc_ref[0], sh_ref[0]).astype(BF16)

    h = h_scr[...]
    a = jnp.dot(h, wa_ref[...], preferred_element_type=F32)
    lin = jnp.dot(h, wl_ref[...], preferred_element_type=F32)

    first = (i % tiles_per_seq) == 0
    prev = halo_scr[j]
    abuf[0:halo, :] = jnp.where(first, jnp.zeros_like(prev), prev)
    abuf[halo:halo + tm, :] = a
    halo_scr[j] = a[tm - halo:tm, :]
    w = cw_ref[...]
    y = w[FFN_CONV - 1:FFN_CONV, :] * a + cb_ref[...]
    for s in range(1, FFN_CONV):
        y = y + w[FFN_CONV - 1 - s:FFN_CONV - s, :] * abuf[halo - s:halo - s + tm, :]
    act = (_silu(y) * lin).astype(BF16)
    part = jnp.dot(act, wd_ref[...], preferred_element_type=F32)

    @pl.when(j == 0)
    def _():
        acc_scr[...] = part

    @pl.when(j > 0)
    def _():
        acc_scr[...] += part

    @pl.when(j == n_ff_tiles - 1)
    def _():
        o_ref[...] = x_ref[...] + gt_ref[0] * acc_scr[...]


def _ffn(x, nw, mod, w_up, conv_w, conv_b, w_down, layer, seq):
    m = x.shape[0]
    tm = 512
    n_ff = 2
    fc = D_FF // n_ff
    tiles_per_seq = seq // tm
    kern = functools.partial(_ffn_kernel, tm=tm, tiles_per_seq=tiles_per_seq, n_ff_tiles=n_ff)
    return pl.pallas_call(
        kern,
        grid=(m // tm, n_ff),
        in_specs=[pl.BlockSpec((tm, D_MODEL), lambda i, j: (i, 0)),
                  pl.BlockSpec((None, 1, D_MODEL), lambda i, j: (layer, 0, 0)),
                  _mod_spec(layer, 4, tiles_per_seq), _mod_spec(layer, 3, tiles_per_seq),
                  _mod_spec(layer, 5, tiles_per_seq),
                  pl.BlockSpec((None, D_MODEL, fc), lambda i, j: (layer, 0, j)),
                  pl.BlockSpec((None, D_MODEL, fc), lambda i, j: (layer, 0, n_ff + j)),
                  pl.BlockSpec((None, FFN_CONV, fc), lambda i, j: (layer, 0, j)),
                  pl.BlockSpec((None, 1, fc), lambda i, j: (layer, 0, j)),
                  pl.BlockSpec((None, fc, D_MODEL), lambda i, j: (layer, j, 0))],
        out_specs=pl.BlockSpec((tm, D_MODEL), lambda i, j: (i, 0)),
        out_shape=jax.ShapeDtypeStruct((m, D_MODEL), F32),
        scratch_shapes=[pltpu.VMEM((tm, D_MODEL), BF16),
                        pltpu.VMEM((tm, D_MODEL), F32),
                        pltpu.VMEM((tm + SUBLANES, fc), F32),
                        pltpu.VMEM((n_ff, SUBLANES, fc), F32)],
        compiler_params=_params("arbitrary", "arbitrary"),
        name="ffn",
    )(x, nw, mod, mod, mod, w_up, w_up, conv_w, conv_b, w_down)


def _pack_w_in(w_in):
    depth = w_in.shape[0]
    o_z, o_a = 3 * DN_WIDTH, 4 * DN_WIDTH
    o_swq = o_a + 2 * DN_HEADS
    o_swk = o_swq + SWA_WIDTH
    o_swv = o_swk + SWA_KV_WIDTH
    o_ga = o_swv + SWA_KV_WIDTH
    o_gb = o_ga + D_MODEL
    del o_z
    pad = jnp.zeros((depth, D_MODEL, IN_PACKED - COL_AB - 2 * DN_HEADS), w_in.dtype)
    packed = jnp.concatenate(
        [w_in[:, :, :o_a], w_in[:, :, o_swq:o_swk], w_in[:, :, o_ga:o_gb], w_in[:, :, o_gb:],
         w_in[:, :, o_swk:o_swv], w_in[:, :, o_swv:o_ga], w_in[:, :, o_a:o_swq], pad], axis=2)
    return packed.astype(BF16)


def _lane_row(v):
    depth, n = v.shape
    return jnp.zeros((depth, 1, LANES), F32).at[:, 0, :n].set(v.astype(F32))


def kernel(x, c, positions, w_ada, b_ada, norm_mix, w_in, dn_conv, dn_a_log, dn_dt_bias, dn_norm,
           w_dn_out, swa_q_norm, swa_k_norm, swa_sinks, w_swa_out, w_o, norm_ffn, w_up, ffn_conv,
           ffn_conv_b, w_down):
    batch, seq, _ = x.shape
    depth = w_ada.shape[0]
    m = batch * seq

    mod_all = _ada_mod(c, w_ada, b_ada)
    cos_t, sin_t = _rope_tables(positions)

    w_in_p = _pack_w_in(w_in)
    w_dn_b, w_sw_b, w_o_b = w_dn_out.astype(BF16), w_swa_out.astype(BF16), w_o.astype(BF16)
    w_up_b, w_down_b = w_up.astype(BF16), w_down.astype(BF16)
    alog = _lane_row(dn_a_log)
    dtb = _lane_row(dn_dt_bias)
    qn = jnp.tile(swa_q_norm, (1, SWA_Q_HEADS)).reshape(depth, 1, SWA_WIDTH)
    kn = jnp.tile(swa_k_norm, (1, SWA_KV_HEADS)).reshape(depth, 1, SWA_KV_WIDTH)

    mod = mod_all.reshape(depth * SUBLANES * 6, 1, D_MODEL)
    norm_mix3 = norm_mix.reshape(depth, 1, D_MODEL)
    norm_ffn3 = norm_ffn.reshape(depth, 1, D_MODEL)
    dn_norm3 = dn_norm.reshape(depth, 1, DN_HEAD_DIM)
    conv_b3 = ffn_conv_b.reshape(depth, 1, D_FF)

    xf = x.reshape(m, D_MODEL)
    for l in range(depth):
        proj, gates = _inproj(xf, norm_mix3, mod, w_in_p, alog, dtb, l, seq)
        o_dn = _deltanet(proj, gates, dn_conv, dn_norm3, l, batch, seq)
        o_sw = _swa(proj, cos_t, sin_t, swa_sinks[l], qn, kn, l, batch, seq)
        xf = _merge(xf, o_dn, o_sw, proj, mod, w_dn_b, w_sw_b, w_o_b, l, seq)
        xf = _ffn(xf, norm_ffn3, mod, w_up_b, ffn_conv, conv_b3, w_down_b, l, seq)
    return xf.reshape(batch, seq, D_MODEL)
```

```python
import functools

import numpy as np
import jax
import jax.numpy as jnp
from jax import lax
from jax.experimental import pallas as pl
from jax.experimental.pallas import tpu as pltpu

F32 = jnp.float32
BF16 = jnp.bfloat16

D_MODEL = 1024
DN_HEADS = 8
DN_HEAD_DIM = 128
DN_WIDTH = DN_HEADS * DN_HEAD_DIM
DN_CONV = 4
DN_CHUNK = 64
SWA_Q_HEADS = 16
SWA_KV_HEADS = 2
SWA_HEAD_DIM = 64
SWA_WIDTH = SWA_Q_HEADS * SWA_HEAD_DIM
SWA_KV_WIDTH = SWA_KV_HEADS * SWA_HEAD_DIM
SWA_BLOCK = 128
SWA_HEAD_BATCH = 16
ROPE_THETA = 500000.0
ROPE_DIM = SWA_HEAD_DIM // 4
ROPE_HALF = ROPE_DIM // 2
D_FF = 2816
FFN_CONV = 3
EPS = 1e-6

LANES = 128
SUBLANES = 8
VMEM_LIMIT = 56 * 1024 * 1024

COL_Q, COL_K, COL_V, COL_Z = 0, 1024, 2048, 3072
COL_SWQ, COL_GA, COL_GB = 4096, 5120, 6144
COL_SWK, COL_SWV, COL_AB = 7168, 7296, 7424
IN_PACKED = 7680
NEG_BIG = -1e30


def _sigmoid(x):
    return 1.0 / (1.0 + jnp.exp(-x))


def _silu(x):
    return x * _sigmoid(x)


def _mm(a, b):
    return jnp.dot(a.astype(BF16), b.astype(BF16), preferred_element_type=F32)


def _mm_nt(a, b):
    return lax.dot_general(a.astype(BF16), b.astype(BF16), (((1,), (1,)), ((), ())),
                           preferred_element_type=F32)


def _params(*sem):
    return pltpu.CompilerParams(dimension_semantics=sem, vmem_limit_bytes=VMEM_LIMIT)


def _ada_kernel(c_ref, w_ref, b_ref, o_ref):
    ca = _silu(c_ref[...])
    o_ref[0] = jnp.dot(ca, w_ref[0], precision=lax.Precision.HIGHEST,
                       preferred_element_type=F32) + b_ref[0]


def _ada_mod(c, w_ada, b_ada):
    depth = w_ada.shape[0]
    batch = c.shape[0]
    n_out = w_ada.shape[2]
    tn = 1536
    c_pad = jnp.zeros((SUBLANES, D_MODEL), F32).at[:batch].set(c)
    return pl.pallas_call(
        _ada_kernel,
        grid=(depth, n_out // tn),
        in_specs=[pl.BlockSpec((SUBLANES, D_MODEL), lambda l, j: (0, 0)),
                  pl.BlockSpec((1, D_MODEL, tn), lambda l, j: (l, 0, j)),
                  pl.BlockSpec((1, 1, tn), lambda l, j: (l, 0, j))],
        out_specs=pl.BlockSpec((1, SUBLANES, tn), lambda l, j: (l, 0, j)),
        out_shape=jax.ShapeDtypeStruct((depth, SUBLANES, n_out), F32),
        compiler_params=_params("arbitrary", "arbitrary"),
        name="ada_mod",
    )(c_pad, w_ada, b_ada.reshape(depth, 1, n_out))


def _rope_kernel(pos_ref, inv_ref, sgn_ref, cos_ref, sin_ref):
    ang = pos_ref[...] * inv_ref[...]
    on = sgn_ref[...] != 0.0
    cos_ref[...] = jnp.where(on, jnp.cos(ang), 1.0)
    sin_ref[...] = jnp.sin(ang) * sgn_ref[...]


def _rope_tables(positions):
    m = positions.size
    tm = min(2048, m)
    lane = np.arange(LANES) % SWA_HEAD_DIM
    inv = np.where(lane < ROPE_DIM,
                   np.power(ROPE_THETA, -(lane % ROPE_HALF).astype(np.float64) / ROPE_HALF), 0.0)
    sgn = np.where(lane < ROPE_HALF, -1.0, np.where(lane < ROPE_DIM, 1.0, 0.0))
    pos = positions.astype(F32).reshape(m, 1)
    return pl.pallas_call(
        _rope_kernel,
        grid=(m // tm,),
        in_specs=[pl.BlockSpec((tm, 1), lambda i: (i, 0)),
                  pl.BlockSpec((1, LANES), lambda i: (0, 0)),
                  pl.BlockSpec((1, LANES), lambda i: (0, 0))],
        out_specs=[pl.BlockSpec((tm, LANES), lambda i: (i, 0)),
                   pl.BlockSpec((tm, LANES), lambda i: (i, 0))],
        out_shape=[jax.ShapeDtypeStruct((m, LANES), F32)] * 2,
        compiler_params=_params("arbitrary"),
        name="rope_tables",
    )(pos, jnp.asarray(inv, F32).reshape(1, LANES), jnp.asarray(sgn, F32).reshape(1, LANES))


def _norm_mod(x, nw, sc, sh):
    ms = jnp.mean(x * x, axis=-1, keepdims=True)
    return (x * lax.rsqrt(ms + EPS) * nw) * (1.0 + sc) + sh


def _inproj_kernel(x_ref, nw_ref, sc_ref, sh_ref, w_ref, alog_ref, dtb_ref,
                   proj_ref, gate_ref, h_scr, *, n_col_tiles, ab_off):
    j = pl.program_id(1)

    @pl.when(j == 0)
    def _():
        h_scr[...] = _norm_mod(x_ref[...], nw_ref[...], sc_ref[0], sh_ref[0]).astype(BF16)

    acc = jnp.dot(h_scr[...], w_ref[...], preferred_element_type=F32)
    proj_ref[...] = acc.astype(BF16)

    @pl.when(j == n_col_tiles - 1)
    def _():
        ab = acc[:, ab_off:ab_off + LANES]
        z = ab + dtb_ref[...]
        softplus = jnp.maximum(z, 0.0) + jnp.log(1.0 + jnp.exp(-jnp.abs(z)))
        g = -jnp.exp(alog_ref[...]) * softplus
        lane = lax.broadcasted_iota(jnp.int32, ab.shape, 1)
        gate_ref[...] = jnp.where(lane < DN_HEADS, g, _sigmoid(ab))


def _mod_spec(layer, k, tiles_per_seq):
    return pl.BlockSpec((1, 1, D_MODEL),
                        lambda i, *_: ((layer * SUBLANES + i // tiles_per_seq) * 6 + k, 0, 0))


def _inproj(x, nw, mod, w, alog, dtb, layer, seq):
    m = x.shape[0]
    tm, tn = 1024, 1536
    nj = IN_PACKED // tn
    tiles_per_seq = seq // tm
    kern = functools.partial(_inproj_kernel, n_col_tiles=nj, ab_off=COL_AB - (nj - 1) * tn)
    return pl.pallas_call(
        kern,
        grid=(m // tm, nj),
        in_specs=[pl.BlockSpec((tm, D_MODEL), lambda i, j: (i, 0)),
                  pl.BlockSpec((None, 1, D_MODEL), lambda i, j: (layer, 0, 0)),
                  _mod_spec(layer, 1, tiles_per_seq), _mod_spec(layer, 0, tiles_per_seq),
                  pl.BlockSpec((None, D_MODEL, tn), lambda i, j: (layer, 0, j)),
                  pl.BlockSpec((None, 1, LANES), lambda i, j: (layer, 0, 0)),
                  pl.BlockSpec((None, 1, LANES), lambda i, j: (layer, 0, 0))],
        out_specs=[pl.BlockSpec((tm, tn), lambda i, j: (i, j)),
                   pl.BlockSpec((tm, LANES), lambda i, j: (i, 0))],
        out_shape=[jax.ShapeDtypeStruct((m, IN_PACKED), BF16),
                   jax.ShapeDtypeStruct((m, LANES), F32)],
        scratch_shapes=[pltpu.VMEM((tm, D_MODEL), BF16)],
        compiler_params=_params("arbitrary", "arbitrary"),
        name="inproj",
    )(x, nw, mod, mod, w, alog, dtb)


DN_BLOCK = 256
INV_BLOCK = 16
DN_PREV_ROWS = 16
DN_HEAD_GROUP = 4
M_INCL, M_STRICT, M_DIAG, M_EYE, M_OFF0 = 0, 1, 2, 3, 4


def _dn_masks():
    r = np.arange(DN_BLOCK)[:, None]
    c = np.arange(DN_BLOCK)[None, :]
    same = lambda b: (r // b) == (c // b)
    chunk = same(DN_CHUNK)
    masks = [chunk & (r >= c), chunk & (r > c), same(INV_BLOCK), r == c]
    b = INV_BLOCK
    while b < DN_CHUNK:
        masks.append(same(2 * b) & ~same(b))
        b *= 2
    return np.stack(masks).astype(np.float32)


def _inv_unit_lower(l_strict, mask_ref, filler):
    dot = functools.partial(jnp.dot, preferred_element_type=F32)
    ds = [l * mask_ref[M_DIAG] for l in l_strict]
    ts = [mask_ref[M_EYE] - d for d in ds]
    dbs = [d.astype(BF16) for d in ds]
    mpows = [dot(db, db) for db in dbs]
    filler()
    n_fac = int(np.log2(INV_BLOCK)) - 1
    for i in range(n_fac):
        mbs = [m.astype(BF16) for m in mpows]
        ts = [t + dot(t.astype(BF16), mb) for t, mb in zip(ts, mbs)]
        if i < n_fac - 1:
            mpows = [dot(mb, mb) for mb in mbs]
        filler()
    n_levels = int(np.log2(DN_CHUNK // INV_BLOCK))
    for lvl in range(n_levels):
        tbs = [t.astype(BF16) for t in ts]
        inner = [dot((l * mask_ref[M_OFF0 + lvl]).astype(BF16), tb).astype(BF16)
                 for l, tb in zip(l_strict, tbs)]
        filler()
        ts = [t - dot(tb, inn) for t, tb, inn in zip(ts, tbs, inner)]
        filler()
    return ts


def _dn_conv_pieces(raw_refs, prev_refs, keep_prev, cw_ref, xbuf, act_ref):
    tb, halo = DN_BLOCK, DN_PREV_ROWS

    def piece(idx, h):
        sl = slice(h * DN_HEAD_DIM, (h + 1) * DN_HEAD_DIM)
        ref, pref = raw_refs[idx], prev_refs[idx]
        if pref is None:
            xbuf[idx, 0:halo, sl] = jnp.zeros((halo, DN_HEAD_DIM), F32)
        else:
            xbuf[idx, 0:halo, sl] = pref[:, sl].astype(F32) * keep_prev
        x = ref[:, sl].astype(F32)
        xbuf[idx, halo:halo + tb, sl] = x
        w = cw_ref[:, idx * DN_WIDTH + h * DN_HEAD_DIM:idx * DN_WIDTH + (h + 1) * DN_HEAD_DIM]
        y = w[DN_CONV - 1:DN_CONV, :] * x
        for s in range(1, DN_CONV):
            y = y + w[DN_CONV - 1 - s:DN_CONV - s, :] * xbuf[idx, halo - s:halo - s + tb, sl]
        y = _silu(y)
        if idx < 2:
            scale = DN_HEAD_DIM ** -0.5 if idx == 0 else 1.0
            y = y * (lax.rsqrt(jnp.sum(y * y, axis=-1, keepdims=True) + EPS) * scale)
        act_ref[idx, :, sl] = y

    return [functools.partial(piece, idx, h) for idx in range(3) for h in range(DN_HEADS)]


def _dn_intra_kernel(q_ref, k_ref, v_ref, qp_ref, kp_ref, vp_ref, gate_ref, cw_ref, mask_ref,
                     u_ref, w_ref, qd_ref, kd_ref, qk_ref, egl_ref, xbuf, act_cur, *, blocks_per_seq):
    i = pl.program_id(0)
    tb, c = DN_BLOCK, DN_CHUNK
    n_chunks = tb // c

    keep_prev = jnp.where((i % blocks_per_seq) == 0, 0.0, 1.0)
    pieces = _dn_conv_pieces((q_ref, k_ref, v_ref), (qp_ref, kp_ref, vp_ref), keep_prev,
                             cw_ref, xbuf, act_cur)
    head_of = [h for _ in range(3) for h in range(DN_HEADS)]
    pending = [p for p, h in zip(pieces, head_of) if h >= DN_HEAD_GROUP]
    for p, h in zip(pieces, head_of):
        if h < DN_HEAD_GROUP:
            p()

    gates = gate_ref[...]
    gcum = jnp.dot(mask_ref[M_INCL], gates, precision=lax.Precision.HIGHEST,
                   preferred_element_type=F32)
    gcum_t = gcum.T
    glast = jnp.concatenate(
        [jnp.broadcast_to(gcum[ci * c + c - 1:ci * c + c, :], (c, LANES)) for ci in range(n_chunks)], axis=0)
    e_cum = jnp.exp(gcum)
    e_rem = jnp.exp(glast - gcum)
    for ci in range(n_chunks):
        gl = gcum_t[0:DN_HEADS, ci * c + c - 1:ci * c + c]
        egl_ref[ci * DN_HEADS:(ci + 1) * DN_HEADS, :] = jnp.broadcast_to(jnp.exp(gl), (DN_HEADS, LANES))

    def filler():
        if pending:
            pending.pop(0)()

    for h0 in range(0, DN_HEADS, DN_HEAD_GROUP):
        heads = range(h0, h0 + DN_HEAD_GROUP)
        sls = [slice(h * DN_HEAD_DIM, (h + 1) * DN_HEAD_DIM) for h in heads]
        rhs, ls = [], []
        if h0 > 0:
            while pending:
                filler()
        for h, sl in zip(heads, sls):
            qh, kh, vh = act_cur[0, :, sl], act_cur[1, :, sl], act_cur[2, :, sl]
            beta = jnp.broadcast_to(gates[:, DN_HEADS + h:DN_HEADS + h + 1], kh.shape)
            eg = jnp.broadcast_to(e_cum[:, h:h + 1], kh.shape)
            kb = kh * beta
            qd_ref[:, sl] = (qh * eg).astype(BF16)
            kd_ref[:, sl] = (kh * e_rem[:, h:h + 1]).astype(BF16)
            rhs.append(jnp.concatenate([vh * beta, kb * eg], axis=1).astype(BF16))
            filler()
            a = _mm_nt(jnp.concatenate([kb, qh], axis=0), kh)
            decay = jnp.exp(jnp.minimum(gcum[:, h:h + 1] - gcum_t[h:h + 1, :], 0.0)) * mask_ref[M_INCL]
            ls.append(a[:tb] * (decay * mask_ref[M_STRICT]))
            qk = a[tb:] * decay
            qk_ref[:, sl] = jnp.concatenate(
                [qk[ci * c:(ci + 1) * c, (ci // 2) * LANES:(ci // 2 + 1) * LANES] for ci in range(n_chunks)],
                axis=0).astype(BF16)
        tinvs = _inv_unit_lower(ls, mask_ref, filler)
        for sl, tinv, r in zip(sls, tinvs, rhs):
            uw = jnp.dot(tinv.astype(BF16), r, preferred_element_type=F32)
            u_ref[:, sl] = uw[:, :DN_HEAD_DIM]
            w_ref[:, sl] = uw[:, DN_HEAD_DIM:].astype(BF16)


def _dn_scan_kernel(u_ref, w_ref, qd_ref, kd_ref, qk_ref, z_ref, egl_ref, nw_ref, o_ref, s_scr, *, tb):
    t = pl.program_id(1)
    c = DN_CHUNK

    @pl.when(t == 0)
    def _():
        s_scr[...] = jnp.zeros_like(s_scr)

    nw = nw_ref[...]
    zeros_v = jnp.zeros((c, DN_HEAD_DIM), BF16)
    heads = range(DN_HEADS)
    sls = [slice(h * DN_HEAD_DIM, (h + 1) * DN_HEAD_DIM) for h in heads]
    dot = functools.partial(jnp.dot, preferred_element_type=F32)

    def out_matmuls(rows, par, ws, vbs):
        pads = [jnp.concatenate([vb, zeros_v] if par == 0 else [zeros_v, vb], axis=0) for vb in vbs]
        return [w_s[c:] + dot(qk_ref[rows, sl], v_pad) for sl, w_s, v_pad in zip(sls, ws, pads)]

    def out_finish(rows, outs):
        for sl, o in zip(sls, outs):
            o = o * lax.rsqrt(jnp.mean(o * o, axis=-1, keepdims=True) + EPS) * nw
            o_ref[rows, sl] = (o * _silu(z_ref[rows, sl].astype(F32))).astype(BF16)

    pending = None
    for ci in range(tb // c):
        rows = slice(ci * c, (ci + 1) * c)
        s_old = [s_scr[h] for h in heads]
        ws = [dot(jnp.concatenate([w_ref[rows, sl], qd_ref[rows, sl]], axis=0), s.astype(BF16))
              for sl, s in zip(sls, s_old)]
        prev_outs = out_matmuls(*pending) if pending else None
        kd_ts = [kd_ref[rows, sl].astype(F32).T.astype(BF16) for sl in sls]
        vbs = [(u_ref[rows, sl] - w_s[:c]).astype(BF16) for sl, w_s in zip(sls, ws)]
        for h, s, vb, kd_t in zip(heads, s_old, vbs, kd_ts):
            egl = egl_ref[ci * DN_HEADS + h:ci * DN_HEADS + h + 1, :]
            s_scr[h] = s * egl + dot(kd_t, vb)
        if pending:
            out_finish(pending[0], prev_outs)
        pending = (rows, ci % 2, ws, vbs)
    out_finish(pending[0], out_matmuls(*pending))


def _deltanet(proj, gates, conv_w, norm_w, layer, batch, seq):
    m = proj.shape[0]
    tb = DN_BLOCK
    nblk = m // tb
    masks = jnp.asarray(_dn_masks())
    prev_per_blk = tb // DN_PREV_ROWS
    cols = (COL_Q // DN_WIDTH, COL_K // DN_WIDTH, COL_V // DN_WIDTH)
    cur = [pl.BlockSpec((tb, DN_WIDTH), lambda i, cb=cb: (i, cb)) for cb in cols]
    prev = [pl.BlockSpec((DN_PREV_ROWS, DN_WIDTH),
                         lambda i, cb=cb: (jnp.maximum(i * prev_per_blk - 1, 0), cb)) for cb in cols]
    tok = pl.BlockSpec((tb, DN_WIDTH), lambda i: (i, 0))
    egl_rows = (tb // DN_CHUNK) * DN_HEADS
    bf_out = jax.ShapeDtypeStruct((m, DN_WIDTH), BF16)
    u, w, qd, kd, qk, egl = pl.pallas_call(
        functools.partial(_dn_intra_kernel, blocks_per_seq=seq // tb),
        grid=(nblk,),
        in_specs=cur + prev + [
            pl.BlockSpec((tb, LANES), lambda i: (i, 0)),
            pl.BlockSpec((None, DN_CONV, 3 * DN_WIDTH), lambda i: (layer, 0, 0)),
            pl.BlockSpec(masks.shape, lambda i: (0, 0, 0))],
        out_specs=[tok, tok, tok, tok, tok, pl.BlockSpec((egl_rows, LANES), lambda i: (i, 0))],
        out_shape=[jax.ShapeDtypeStruct((m, DN_WIDTH), F32), bf_out, bf_out, bf_out, bf_out,
                   jax.ShapeDtypeStruct((m // DN_CHUNK * DN_HEADS, LANES), F32)],
        scratch_shapes=[pltpu.VMEM((3, tb + DN_PREV_ROWS, DN_WIDTH), F32),
                        pltpu.VMEM((3, tb, DN_WIDTH), F32)],
        compiler_params=_params("parallel"),
        name="dn_intra",
    )(*([proj] * 6), gates, conv_w, masks)

    ts = 256
    nt = seq // ts
    blk = lambda cb: pl.BlockSpec((ts, DN_WIDTH), lambda b, t: (b * nt + t, cb))
    return pl.pallas_call(
        functools.partial(_dn_scan_kernel, tb=ts),
        grid=(batch, nt),
        in_specs=[blk(0), blk(0), blk(0), blk(0), blk(0), blk(COL_Z // DN_WIDTH),
                  pl.BlockSpec((ts // DN_CHUNK * DN_HEADS, LANES), lambda b, t: (b * nt + t, 0)),
                  pl.BlockSpec((None, 1, DN_HEAD_DIM), lambda b, t: (layer, 0, 0))],
        out_specs=blk(0),
        out_shape=bf_out,
        scratch_shapes=[pltpu.VMEM((DN_HEADS, DN_HEAD_DIM, DN_HEAD_DIM), F32)],
        compiler_params=_params("arbitrary", "arbitrary"),
        name="dn_scan",
    )(u, w, qd, kd, qk, proj, egl, norm_w)


SWA_TILE = 2 * LANES
LOG2E = float(np.log2(np.e))


def _swa_consts():
    r = np.arange(SWA_TILE)[:, None]
    c = np.arange(SWA_TILE)[None, :]
    ones = (r // SWA_HEAD_DIM) == (c // SWA_HEAD_DIM)
    cl = c % SWA_HEAD_DIM
    perm = ((cl < ROPE_HALF) & (r == c + ROPE_HALF)) | ((cl >= ROPE_HALF) & (cl < ROPE_DIM) & (r == c - ROPE_HALF))
    return np.stack([ones, perm]).astype(np.float32)


def _norm_rope_tiles(tiles, nws, cos_t, sin_t, const_ref):
    dot = functools.partial(jnp.dot, preferred_element_type=F32)
    ws = [t.shape[1] for t in tiles]
    ms = [dot((t * t).astype(BF16), const_ref[0, :w, :w]) for t, w in zip(tiles, ws)]
    ys = [t * lax.rsqrt(m * (1.0 / SWA_HEAD_DIM) + EPS) * nw for t, m, nw in zip(tiles, ms, nws)]
    partners = [dot(y.astype(BF16), const_ref[1, :w, :w]) for y, w in zip(ys, ws)]
    return [y * cos_t[:, :w] + p * sin_t[:, :w] for y, p, w in zip(ys, partners, ws)]


def _swa_kernel(sink_ref, q_ref, k_ref, v_ref, cos_ref, sin_ref, qn_ref, kn_ref, const_ref, o_ref,
                kprev, vprev):
    n = pl.program_id(1)
    blk = SWA_BLOCK

    @pl.when(n == 0)
    def _():
        kprev[...] = jnp.zeros_like(kprev)
        vprev[...] = jnp.zeros_like(vprev)

    cos_t = jnp.tile(cos_ref[...], (1, SWA_TILE // LANES))
    sin_t = jnp.tile(sin_ref[...], (1, SWA_TILE // LANES))
    n_qt = SWA_WIDTH // SWA_TILE
    tiles = [q_ref[:, t * SWA_TILE:(t + 1) * SWA_TILE].astype(F32) for t in range(n_qt)] + [k_ref[...].astype(F32)]
    nws = [qn_ref[:, t * SWA_TILE:(t + 1) * SWA_TILE] for t in range(n_qt)] + [kn_ref[...]]
    roped = _norm_rope_tiles(tiles, nws, cos_t, sin_t, const_ref)
    qb = jnp.concatenate([(t * (SWA_HEAD_DIM ** -0.5 * LOG2E)).astype(BF16) for t in roped[:n_qt]], axis=1)

    low = lax.broadcasted_iota(jnp.int32, (blk, LANES), 1) < SWA_HEAD_DIM

    def split_heads(cur, prev_ref):
        swap = pltpu.roll(cur, SWA_HEAD_DIM, axis=1)
        parts = [jnp.where(low, cur, 0.0), jnp.where(low, swap, 0.0),
                 jnp.where(low, 0.0, swap), jnp.where(low, 0.0, cur)]
        bands = []
        for idx, part in enumerate(parts):
            part = part.astype(BF16)
            bands.append(jnp.concatenate([prev_ref[idx], part], axis=0))
            prev_ref[idx] = part
        return bands

    k_bands = split_heads(roped[n_qt], kprev)
    v_bands = split_heads(v_ref[...].astype(F32), vprev)

    qi = lax.broadcasted_iota(jnp.int32, (blk, 2 * blk), 0)
    kj = lax.broadcasted_iota(jnp.int32, (blk, 2 * blk), 1)
    rel = qi + blk - kj
    bias = jnp.where(rel >= 0, jnp.where(rel < blk, 0.0, NEG_BIG), NEG_BIG)
    prev_penalty = jnp.where(n > 0, 0.0, NEG_BIG)
    bias = bias + jnp.where(kj < blk, prev_penalty, 0.0)

    group = SWA_Q_HEADS // SWA_KV_HEADS
    for h0 in range(0, SWA_Q_HEADS, SWA_HEAD_BATCH):
        heads = range(h0, h0 + SWA_HEAD_BATCH)
        kops = [k_bands[2 * (h % 2) + h // group] for h in heads]
        vops = [v_bands[2 * (h % 2) + h // group] for h in heads]
        sinks = [sink_ref[h] * LOG2E for h in heads]
        ss = [_mm_nt(qb[:, (h // 2) * LANES:(h // 2 + 1) * LANES], kop) + bias
              for h, kop in zip(heads, kops)]
        mxs = [jnp.maximum(jnp.max(s, axis=-1, keepdims=True), sink) for s, sink in zip(ss, sinks)]
        ps = [jnp.exp2(s - mx) for s, mx in zip(ss, mxs)]
        denoms = [jnp.sum(p, axis=-1, keepdims=True) + jnp.exp2(sink - mx)
                  for p, sink, mx in zip(ps, sinks, mxs)]
        outs = [_mm(p, vop) * (1.0 / d) for p, vop, d in zip(ps, vops, denoms)]
        for pair in range(h0 // 2, (h0 + SWA_HEAD_BATCH) // 2):
            o_ref[:, pair * LANES:(pair + 1) * LANES] = (
                outs[2 * pair - h0] + outs[2 * pair + 1 - h0]).astype(BF16)


def _swa(proj, cos_t, sin_t, sinks, qn, kn, layer, batch, seq):
    m = proj.shape[0]
    blk = SWA_BLOCK
    nb = seq // blk
    row = lambda b, n: b * nb + n
    consts = jnp.asarray(_swa_consts(), BF16)
    return pl.pallas_call(
        _swa_kernel,
        grid=(batch, nb),
        in_specs=[pl.BlockSpec(memory_space=pltpu.SMEM),
                  pl.BlockSpec((blk, SWA_WIDTH), lambda b, n: (row(b, n), COL_SWQ // SWA_WIDTH)),
                  pl.BlockSpec((blk, SWA_KV_WIDTH), lambda b, n: (row(b, n), COL_SWK // SWA_KV_WIDTH)),
                  pl.BlockSpec((blk, SWA_KV_WIDTH), lambda b, n: (row(b, n), COL_SWV // SWA_KV_WIDTH)),
                  pl.BlockSpec((blk, LANES), lambda b, n: (row(b, n), 0)),
                  pl.BlockSpec((blk, LANES), lambda b, n: (row(b, n), 0)),
                  pl.BlockSpec((None, 1, SWA_WIDTH), lambda b, n: (layer, 0, 0)),
                  pl.BlockSpec((None, 1, SWA_KV_WIDTH), lambda b, n: (layer, 0, 0)),
                  pl.BlockSpec(consts.shape, lambda b, n: (0, 0, 0))],
        out_specs=pl.BlockSpec((blk, SWA_WIDTH), lambda b, n: (row(b, n), 0)),
        out_shape=jax.ShapeDtypeStruct((m, SWA_WIDTH), BF16),
        scratch_shapes=[pltpu.VMEM((2 * SWA_KV_HEADS, blk, SWA_KV_WIDTH), BF16),
                        pltpu.VMEM((2 * SWA_KV_HEADS, blk, SWA_KV_WIDTH), BF16)],
        compiler_params=_params("arbitrary", "arbitrary"),
        name="swa",
    )(sinks, proj, proj, proj, cos_t, sin_t, qn, kn, consts)


def _merge_kernel(x_ref, odn_ref, osw_ref, ga_ref, gb_ref, gt_ref, wdn_ref, wsw_ref, wo_ref, o_ref):
    ya = jnp.dot(odn_ref[...], wdn_ref[...], preferred_element_type=F32)
    yb = jnp.dot(osw_ref[...], wsw_ref[...], preferred_element_type=F32)
    merged = _sigmoid(ga_ref[...].astype(F32)) * ya + _sigmoid(gb_ref[...].astype(F32)) * yb
    out = jnp.dot(merged.astype(BF16), wo_ref[...], preferred_element_type=F32)
    o_ref[...] = x_ref[...] + gt_ref[0] * out


def _merge(x, o_dn, o_sw, proj, mod, w_dn, w_sw, w_o, layer, seq):
    m = x.shape[0]
    tm = 512
    tiles_per_seq = seq // tm
    tok = lambda cb: pl.BlockSpec((tm, D_MODEL), lambda i: (i, cb))
    wfull = pl.BlockSpec((None, D_MODEL, D_MODEL), lambda i: (layer, 0, 0))
    return pl.pallas_call(
        _merge_kernel,
        grid=(m // tm,),
        in_specs=[tok(0), tok(0), tok(0), tok(COL_GA // D_MODEL), tok(COL_GB // D_MODEL),
                  _mod_spec(layer, 2, tiles_per_seq), wfull, wfull, wfull],
        out_specs=tok(0),
        out_shape=jax.ShapeDtypeStruct((m, D_MODEL), F32),
        compiler_params=_params("arbitrary"),
        name="merge_out",
    )(x, o_dn, o_sw, proj, proj, mod, w_dn, w_sw, w_o)


def _ffn_kernel(x_ref, nw_ref, sc_ref, sh_ref, gt_ref, wa_ref, wl_ref, cw_ref, cb_ref, wd_ref,
                o_ref, h_scr, acc_scr, abuf, halo_scr, *, tm, tiles_per_seq, n_ff_tiles):
    i = pl.program_id(0)
    j = pl.program_id(1)
    halo = SUBLANES

    @pl.when(j == 0)
    def _():
        h_scr[...] = _norm_mod(x_ref[...], nw_ref[...], sc_ref[0], sh_ref[0]).astype(BF16)

    h = h_scr[...]
    a = jnp.dot(h, wa_ref[...], preferred_element_type=F32)
    lin = jnp.dot(h, wl_ref[...], preferred_element_type=F32)

    first = (i % tiles_per_seq) == 0
    prev = halo_scr[j]
    abuf[0:halo, :] = jnp.where(first, jnp.zeros_like(prev), prev)
    abuf[halo:halo + tm, :] = a
    halo_scr[j] = a[tm - halo:tm, :]
    w = cw_ref[...]
    y = w[FFN_CONV - 1:FFN_CONV, :] * a + cb_ref[...]
    for s in range(1, FFN_CONV):
        y = y + w[FFN_CONV - 1 - s:FFN_CONV - s, :] * abuf[halo - s:halo - s + tm, :]
    act = (_silu(y) * lin).astype(BF16)
    part = jnp.dot(act, wd_ref[...], preferred_element_type=F32)

    @pl.when(j == 0)
    def _():
        acc_scr[...] = part

    @pl.when(j > 0)
    def _():
        acc_scr[...] += part

    @pl.when(j == n_ff_tiles - 1)
    def _():
        o_ref[...] = x_ref[...] + gt_ref[0] * acc_scr[...]


def _ffn(x, nw, mod, w_up, conv_w, conv_b, w_down, layer, seq):
    m = x.shape[0]
    tm = 512
    n_ff = 2
    fc = D_FF // n_ff
    tiles_per_seq = seq // tm
    kern = functools.partial(_ffn_kernel, tm=tm, tiles_per_seq=tiles_per_seq, n_ff_tiles=n_ff)
    return pl.pallas_call(
        kern,
        grid=(m // tm, n_ff),
        in_specs=[pl.BlockSpec((tm, D_MODEL), lambda i, j: (i, 0)),
                  pl.BlockSpec((None, 1, D_MODEL), lambda i, j: (layer, 0, 0)),
                  _mod_spec(layer, 4, tiles_per_seq), _mod_spec(layer, 3, tiles_per_seq),
                  _mod_spec(layer, 5, tiles_per_seq),
                  pl.BlockSpec((None, D_MODEL, fc), lambda i, j: (layer, 0, j)),
                  pl.BlockSpec((None, D_MODEL, fc), lambda i, j: (layer, 0, n_ff + j)),
                  pl.BlockSpec((None, FFN_CONV, fc), lambda i, j: (layer, 0, j)),
                  pl.BlockSpec((None, 1, fc), lambda i, j: (layer, 0, j)),
                  pl.BlockSpec((None, fc, D_MODEL), lambda i, j: (layer, j, 0))],
        out_specs=pl.BlockSpec((tm, D_MODEL), lambda i, j: (i, 0)),
        out_shape=jax.ShapeDtypeStruct((m, D_MODEL), F32),
        scratch_shapes=[pltpu.VMEM((tm, D_MODEL), BF16),
                        pltpu.VMEM((tm, D_MODEL), F32),
                        pltpu.VMEM((tm + SUBLANES, fc), F32),
                        pltpu.VMEM((n_ff, SUBLANES, fc), F32)],
        compiler_params=_params("arbitrary", "arbitrary"),
        name="ffn",
    )(x, nw, mod, mod, mod, w_up, w_up, conv_w, conv_b, w_down)


def _pack_w_in(w_in):
    depth = w_in.shape[0]
    o_z, o_a = 3 * DN_WIDTH, 4 * DN_WIDTH
    o_swq = o_a + 2 * DN_HEADS
    o_swk = o_swq + SWA_WIDTH
    o_swv = o_swk + SWA_KV_WIDTH
    o_ga = o_swv + SWA_KV_WIDTH
    o_gb = o_ga + D_MODEL
    del o_z
    w = w_in.astype(BF16)
    pad = jnp.zeros((depth, D_MODEL, IN_PACKED - COL_AB - 2 * DN_HEADS), BF16)
    return jnp.concatenate(
        [w[:, :, :o_a], w[:, :, o_swq:o_swk], w[:, :, o_ga:o_gb], w[:, :, o_gb:],
         w[:, :, o_swk:o_swv], w[:, :, o_swv:o_ga], w[:, :, o_a:o_swq], pad], axis=2)


def _lane_row(v):
    depth, n = v.shape
    return jnp.zeros((depth, 1, LANES), F32).at[:, 0, :n].set(v.astype(F32))


def kernel(x, c, positions, w_ada, b_ada, norm_mix, w_in, dn_conv, dn_a_log, dn_dt_bias, dn_norm,
           w_dn_out, swa_q_norm, swa_k_norm, swa_sinks, w_swa_out, w_o, norm_ffn, w_up, ffn_conv,
           ffn_conv_b, w_down):
    batch, seq, _ = x.shape
    depth = w_ada.shape[0]
    m = batch * seq

    mod_all = _ada_mod(c, w_ada, b_ada)
    cos_t, sin_t = _rope_tables(positions)

    w_in_p = _pack_w_in(w_in)
    w_dn_b, w_sw_b, w_o_b = w_dn_out.astype(BF16), w_swa_out.astype(BF16), w_o.astype(BF16)
    w_up_b, w_down_b = w_up.astype(BF16), w_down.astype(BF16)
    alog = _lane_row(dn_a_log)
    dtb = _lane_row(dn_dt_bias)
    qn = jnp.tile(swa_q_norm, (1, SWA_Q_HEADS)).reshape(depth, 1, SWA_WIDTH)
    kn = jnp.tile(swa_k_norm, (1, SWA_KV_HEADS)).reshape(depth, 1, SWA_KV_WIDTH)

    mod = mod_all.reshape(depth * SUBLANES * 6, 1, D_MODEL)
    norm_mix3 = norm_mix.reshape(depth, 1, D_MODEL)
    norm_ffn3 = norm_ffn.reshape(depth, 1, D_MODEL)
    dn_norm3 = dn_norm.reshape(depth, 1, DN_HEAD_DIM)
    conv_b3 = ffn_conv_b.reshape(depth, 1, D_FF)

    xf = x.reshape(m, D_MODEL)
    for l in range(depth):
        proj, gates = _inproj(xf, norm_mix3, mod, w_in_p, alog, dtb, l, seq)
        o_dn = _deltanet(proj, gates, dn_conv, dn_norm3, l, batch, seq)
        o_sw = _swa(proj, cos_t, sin_t, swa_sinks[l], qn, kn, l, batch, seq)
        xf = _merge(xf, o_dn, o_sw, proj, mod, w_dn_b, w_sw_b, w_o_b, l, seq)
        xf = _ffn(xf, norm_ffn3, mod, w_up_b, ffn_conv, conv_b3, w_down_b, l, seq)
    return xf.reshape(batch, seq, D_MODEL)
```

```python
import functools

import numpy as np
import jax
import jax.numpy as jnp
from jax import lax
from jax.experimental import pallas as pl
from jax.experimental.pallas import tpu as pltpu

F32 = jnp.float32
BF16 = jnp.bfloat16

D_MODEL = 1024
DN_HEADS = 8
DN_HEAD_DIM = 128
DN_WIDTH = DN_HEADS * DN_HEAD_DIM
DN_CONV = 4
DN_CHUNK = 64
SWA_Q_HEADS = 16
SWA_KV_HEADS = 2
SWA_HEAD_DIM = 64
SWA_WIDTH = SWA_Q_HEADS * SWA_HEAD_DIM
SWA_KV_WIDTH = SWA_KV_HEADS * SWA_HEAD_DIM
SWA_BLOCK = 128
SWA_HEAD_BATCH = 16
ROPE_THETA = 500000.0
ROPE_DIM = SWA_HEAD_DIM // 4
ROPE_HALF = ROPE_DIM // 2
D_FF = 2816
FFN_CONV = 3
EPS = 1e-6

LANES = 128
SUBLANES = 8
VMEM_LIMIT = 56 * 1024 * 1024

COL_Q, COL_K, COL_V, COL_Z = 0, 1024, 2048, 3072
COL_SWQ, COL_GA, COL_GB = 4096, 5120, 6144
COL_SWK, COL_SWV, COL_AB = 7168, 7296, 7424
IN_PACKED = 7680
NEG_BIG = -1e30


def _sigmoid(x):
    return 1.0 / (1.0 + jnp.exp(-x))


def _silu(x):
    return x * _sigmoid(x)


def _mm(a, b):
    return jnp.dot(a.astype(BF16), b.astype(BF16), preferred_element_type=F32)


def _mm_nt(a, b):
    return lax.dot_general(a.astype(BF16), b.astype(BF16), (((1,), (1,)), ((), ())),
                           preferred_element_type=F32)


def _params(*sem):
    return pltpu.CompilerParams(dimension_semantics=sem, vmem_limit_bytes=VMEM_LIMIT)


def _ada_kernel(c_ref, w_ref, b_ref, o_ref):
    ca = _silu(c_ref[...])
    o_ref[0] = jnp.dot(ca, w_ref[0], precision=lax.Precision.HIGHEST,
                       preferred_element_type=F32) + b_ref[0]


def _ada_mod(c, w_ada, b_ada):
    depth = w_ada.shape[0]
    batch = c.shape[0]
    n_out = w_ada.shape[2]
    tn = 1536
    c_pad = jnp.zeros((SUBLANES, D_MODEL), F32).at[:batch].set(c)
    return pl.pallas_call(
        _ada_kernel,
        grid=(depth, n_out // tn),
        in_specs=[pl.BlockSpec((SUBLANES, D_MODEL), lambda l, j: (0, 0)),
                  pl.BlockSpec((1, D_MODEL, tn), lambda l, j: (l, 0, j)),
                  pl.BlockSpec((1, 1, tn), lambda l, j: (l, 0, j))],
        out_specs=pl.BlockSpec((1, SUBLANES, tn), lambda l, j: (l, 0, j)),
        out_shape=jax.ShapeDtypeStruct((depth, SUBLANES, n_out), F32),
        compiler_params=_params("arbitrary", "arbitrary"),
        name="ada_mod",
    )(c_pad, w_ada, b_ada.reshape(depth, 1, n_out))


def _rope_kernel(pos_ref, inv_ref, sgn_ref, cos_ref, sin_ref):
    ang = pos_ref[...] * inv_ref[...]
    on = sgn_ref[...] != 0.0
    cos_ref[...] = jnp.where(on, jnp.cos(ang), 1.0)
    sin_ref[...] = jnp.sin(ang) * sgn_ref[...]


def _rope_tables(positions):
    m = positions.size
    tm = min(2048, m)
    lane = np.arange(LANES) % SWA_HEAD_DIM
    inv = np.where(lane < ROPE_DIM,
                   np.power(ROPE_THETA, -(lane % ROPE_HALF).astype(np.float64) / ROPE_HALF), 0.0)
    sgn = np.where(lane < ROPE_HALF, -1.0, np.where(lane < ROPE_DIM, 1.0, 0.0))
    pos = positions.astype(F32).reshape(m, 1)
    return pl.pallas_call(
        _rope_kernel,
        grid=(m // tm,),
        in_specs=[pl.BlockSpec((tm, 1), lambda i: (i, 0)),
                  pl.BlockSpec((1, LANES), lambda i: (0, 0)),
                  pl.BlockSpec((1, LANES), lambda i: (0, 0))],
        out_specs=[pl.BlockSpec((tm, LANES), lambda i: (i, 0)),
                   pl.BlockSpec((tm, LANES), lambda i: (i, 0))],
        out_shape=[jax.ShapeDtypeStruct((m, LANES), F32)] * 2,
        compiler_params=_params("arbitrary"),
        name="rope_tables",
    )(pos, jnp.asarray(inv, F32).reshape(1, LANES), jnp.asarray(sgn, F32).reshape(1, LANES))


def _norm_mod(x, nw, sc, sh):
    ms = jnp.mean(x * x, axis=-1, keepdims=True)
    return (x * lax.rsqrt(ms + EPS) * nw) * (1.0 + sc) + sh


def _inproj_kernel(x_ref, nw_ref, sc_ref, sh_ref, w_ref, alog_ref, dtb_ref,
                   proj_ref, gate_ref, h_scr, *, n_col_tiles, ab_off):
    j = pl.program_id(1)

    @pl.when(j == 0)
    def _():
        h_scr[...] = _norm_mod(x_ref[...], nw_ref[...], sc_ref[0], sh_ref[0]).astype(BF16)

    acc = jnp.dot(h_scr[...], w_ref[...], preferred_element_type=F32)
    proj_ref[...] = acc.astype(BF16)

    @pl.when(j == n_col_tiles - 1)
    def _():
        ab = acc[:, ab_off:ab_off + LANES]
        z = ab + dtb_ref[...]
        softplus = jnp.maximum(z, 0.0) + jnp.log(1.0 + jnp.exp(-jnp.abs(z)))
        g = -jnp.exp(alog_ref[...]) * softplus
        lane = lax.broadcasted_iota(jnp.int32, ab.shape, 1)
        gate_ref[...] = jnp.where(lane < DN_HEADS, g, _sigmoid(ab))


def _mod_spec(layer, k, tiles_per_seq):
    return pl.BlockSpec((1, 1, D_MODEL),
                        lambda i, *_: ((layer * SUBLANES + i // tiles_per_seq) * 6 + k, 0, 0))


def _inproj(x, nw, mod, w, alog, dtb, layer, seq):
    m = x.shape[0]
    tm, tn = 1024, 1536
    nj = IN_PACKED // tn
    tiles_per_seq = seq // tm
    kern = functools.partial(_inproj_kernel, n_col_tiles=nj, ab_off=COL_AB - (nj - 1) * tn)
    return pl.pallas_call(
        kern,
        grid=(m // tm, nj),
        in_specs=[pl.BlockSpec((tm, D_MODEL), lambda i, j: (i, 0)),
                  pl.BlockSpec((None, 1, D_MODEL), lambda i, j: (layer, 0, 0)),
                  _mod_spec(layer, 1, tiles_per_seq), _mod_spec(layer, 0, tiles_per_seq),
                  pl.BlockSpec((None, D_MODEL, tn), lambda i, j: (layer, 0, j)),
                  pl.BlockSpec((None, 1, LANES), lambda i, j: (layer, 0, 0)),
                  pl.BlockSpec((None, 1, LANES), lambda i, j: (layer, 0, 0))],
        out_specs=[pl.BlockSpec((tm, tn), lambda i, j: (i, j)),
                   pl.BlockSpec((tm, LANES), lambda i, j: (i, 0))],
        out_shape=[jax.ShapeDtypeStruct((m, IN_PACKED), BF16),
                   jax.ShapeDtypeStruct((m, LANES), F32)],
        scratch_shapes=[pltpu.VMEM((tm, D_MODEL), BF16)],
        compiler_params=_params("arbitrary", "arbitrary"),
        name="inproj",
    )(x, nw, mod, mod, w, alog, dtb)


DN_BLOCK = 256
DN_UNIT = 128
INV_BLOCK = 16
DN_PREV_ROWS = 16
DN_HEAD_GROUP = 8
M_INCL, M_STRICT, M_DIAG, M_EYE, M_NEG_INCL, M_OFF0 = 0, 1, 2, 3, 4, 5
LOG2E = float(np.log2(np.e))


def _dn_masks():
    r = np.arange(DN_UNIT)[:, None]
    c = np.arange(DN_UNIT)[None, :]
    same = lambda b: (r // b) == (c // b)
    chunk = same(DN_CHUNK)
    incl = chunk & (r >= c)
    masks = [incl, chunk & (r > c), same(INV_BLOCK), r == c, np.where(incl, 0.0, NEG_BIG)]
    b = INV_BLOCK
    while b < DN_CHUNK:
        masks.append(same(2 * b) & ~same(b))
        b *= 2
    return np.stack(masks).astype(np.float32)


def _inv_unit_lower(l_strict, mask_ref, filler):
    dot = functools.partial(jnp.dot, preferred_element_type=F32)
    ds = [l * mask_ref[M_DIAG] for l in l_strict]
    ts = [mask_ref[M_EYE] - d for d in ds]
    dbs = [d.astype(BF16) for d in ds]
    mpows = [dot(db, db) for db in dbs]
    filler()
    n_fac = int(np.log2(INV_BLOCK)) - 1
    for i in range(n_fac):
        mbs = [m.astype(BF16) for m in mpows]
        ts = [t + dot(t.astype(BF16), mb) for t, mb in zip(ts, mbs)]
        if i < n_fac - 1:
            mpows = [dot(mb, mb) for mb in mbs]
        filler()
    n_levels = int(np.log2(DN_CHUNK // INV_BLOCK))
    for lvl in range(n_levels):
        tbs = [t.astype(BF16) for t in ts]
        inner = [dot((l * mask_ref[M_OFF0 + lvl]).astype(BF16), tb).astype(BF16)
                 for l, tb in zip(l_strict, tbs)]
        filler()
        ts = [t - dot(tb, inn) for t, tb, inn in zip(ts, tbs, inner)]
        filler()
    return ts


def _dn_conv_pieces(raw_refs, prev_refs, keep_prev, cw_ref, xbuf, act_ref):
    tb, halo = DN_BLOCK, DN_PREV_ROWS

    def piece(idx, h):
        sl = slice(h * DN_HEAD_DIM, (h + 1) * DN_HEAD_DIM)
        ref, pref = raw_refs[idx], prev_refs[idx]
        if pref is None:
            xbuf[idx, 0:halo, sl] = jnp.zeros((halo, DN_HEAD_DIM), F32)
        else:
            xbuf[idx, 0:halo, sl] = pref[:, sl].astype(F32) * keep_prev
        x = ref[:, sl].astype(F32)
        xbuf[idx, halo:halo + tb, sl] = x
        w = cw_ref[:, idx * DN_WIDTH + h * DN_HEAD_DIM:idx * DN_WIDTH + (h + 1) * DN_HEAD_DIM]
        y = w[DN_CONV - 1:DN_CONV, :] * x
        for s in range(1, DN_CONV):
            y = y + w[DN_CONV - 1 - s:DN_CONV - s, :] * xbuf[idx, halo - s:halo - s + tb, sl]
        y = _silu(y)
        if idx < 2:
            scale = DN_HEAD_DIM ** -0.5 if idx == 0 else 1.0
            y = y * (lax.rsqrt(jnp.sum(y * y, axis=-1, keepdims=True) + EPS) * scale)
        act_ref[idx, :, sl] = y

    return [functools.partial(piece, idx, h) for idx in range(3) for h in range(DN_HEADS)]


def _dn_intra_kernel(q_ref, k_ref, v_ref, qp_ref, kp_ref, vp_ref, gate_ref, cw_ref, mask_ref,
                     u_ref, w_ref, qd_ref, kd_ref, qk_ref, egl_ref, xbuf, act_cur, *, blocks_per_seq):
    i = pl.program_id(0)
    tb, c = DN_BLOCK, DN_CHUNK
    n_chunks = tb // c

    keep_prev = jnp.where((i % blocks_per_seq) == 0, 0.0, 1.0)
    pieces = _dn_conv_pieces((q_ref, k_ref, v_ref), (qp_ref, kp_ref, vp_ref), keep_prev,
                             cw_ref, xbuf, act_cur)
    head_of = [h for _ in range(3) for h in range(DN_HEADS)]
    pending = [p for p, h in zip(pieces, head_of) if h >= DN_HEAD_GROUP]
    for p, h in zip(pieces, head_of):
        if h < DN_HEAD_GROUP:
            p()

    un = DN_UNIT
    units = [slice(p * un, (p + 1) * un) for p in range(tb // un)]
    gates = gate_ref[...]
    gcum = jnp.concatenate(
        [jnp.dot(mask_ref[M_INCL], gates[rows], precision=lax.Precision.HIGHEST, preferred_element_type=F32)
         for rows in units], axis=0)
    gcum_t = gcum.T
    gcum2 = gcum * LOG2E
    gcum2_t = gcum_t * LOG2E
    glast = jnp.concatenate(
        [jnp.broadcast_to(gcum[ci * c + c - 1:ci * c + c, :], (c, LANES)) for ci in range(n_chunks)], axis=0)
    e_cum = jnp.exp(gcum)
    e_rem = jnp.exp(glast - gcum)
    for ci in range(n_chunks):
        gl = gcum_t[0:DN_HEADS, ci * c + c - 1:ci * c + c]
        egl_ref[ci * DN_HEADS:(ci + 1) * DN_HEADS, :] = jnp.broadcast_to(jnp.exp(gl), (DN_HEADS, LANES))

    def head_setup(h, rhs, ls):
        sl = slice(h * DN_HEAD_DIM, (h + 1) * DN_HEAD_DIM)
        qh, kh, vh = act_cur[0, :, sl], act_cur[1, :, sl], act_cur[2, :, sl]
        beta = jnp.broadcast_to(gates[:, DN_HEADS + h:DN_HEADS + h + 1], kh.shape)
        eg = jnp.broadcast_to(e_cum[:, h:h + 1], kh.shape)
        kb = kh * beta
        qd_ref[:, sl] = (qh * eg).astype(BF16)
        kd_ref[:, sl] = (kh * e_rem[:, h:h + 1]).astype(BF16)
        rhs_h = jnp.concatenate([vh * beta, kb * eg], axis=1).astype(BF16)
        for rows in units:
            rhs.append(rhs_h[rows])
            a = _mm_nt(jnp.concatenate([kb[rows], qh[rows]], axis=0), kh[rows])
            decay = jnp.exp2(gcum2[rows, h:h + 1] - gcum2_t[h:h + 1, rows] + mask_ref[M_NEG_INCL])
            ls.append(a[:un] * (decay * mask_ref[M_STRICT]))
            qk_ref[rows, sl] = (a[un:] * decay).astype(BF16)

    def filler():
        if pending:
            pending.pop(0)()

    for h0 in range(0, DN_HEADS, DN_HEAD_GROUP):
        heads = range(h0, h0 + DN_HEAD_GROUP)
        rhs, ls = [], []
        while h0 > 0 and pending:
            filler()
        for h in heads:
            head_setup(h, rhs, ls)
            filler()
        tinvs = _inv_unit_lower(ls, mask_ref, filler)
        where = [(h, rows) for h in heads for rows in units]
        for (h, rows), tinv, r in zip(where, tinvs, rhs):
            sl = slice(h * DN_HEAD_DIM, (h + 1) * DN_HEAD_DIM)
            uw = jnp.dot(tinv.astype(BF16), r, preferred_element_type=F32)
            u_ref[rows, sl] = uw[:, :DN_HEAD_DIM]
            w_ref[rows, sl] = uw[:, DN_HEAD_DIM:].astype(BF16)


def _dn_scan_kernel(u_ref, w_ref, qd_ref, kd_ref, qk_ref, z_ref, egl_ref, nw_ref, o_ref, s_scr, *, tb):
    t = pl.program_id(1)
    c = DN_CHUNK

    @pl.when(t == 0)
    def _():
        s_scr[...] = jnp.zeros_like(s_scr)

    nw = nw_ref[...]
    zeros_v = jnp.zeros((c, DN_HEAD_DIM), BF16)
    heads = range(DN_HEADS)
    sls = [slice(h * DN_HEAD_DIM, (h + 1) * DN_HEAD_DIM) for h in heads]
    dot = functools.partial(jnp.dot, preferred_element_type=F32)

    def out_matmuls(rows, par, ws, vbs):
        pads = [jnp.concatenate([vb, zeros_v] if par == 0 else [zeros_v, vb], axis=0) for vb in vbs]
        return [w_s[c:] + dot(qk_ref[rows, sl], v_pad) for sl, w_s, v_pad in zip(sls, ws, pads)]

    def out_finish(rows, outs):
        for sl, o in zip(sls, outs):
            o = o * lax.rsqrt(jnp.mean(o * o, axis=-1, keepdims=True) + EPS) * nw
            o_ref[rows, sl] = (o * _silu(z_ref[rows, sl].astype(F32))).astype(BF16)

    pending = None
    for ci in range(tb // c):
        rows = slice(ci * c, (ci + 1) * c)
        s_old = [s_scr[h] for h in heads]
        ws = [dot(jnp.concatenate([w_ref[rows, sl], qd_ref[rows, sl]], axis=0), s.astype(BF16))
              for sl, s in zip(sls, s_old)]
        prev_outs = out_matmuls(*pending) if pending else None
        kd_ts = [kd_ref[rows, sl].astype(F32).T.astype(BF16) for sl in sls]
        vbs = [(u_ref[rows, sl] - w_s[:c]).astype(BF16) for sl, w_s in zip(sls, ws)]
        for h, s, vb, kd_t in zip(heads, s_old, vbs, kd_ts):
            egl = egl_ref[ci * DN_HEADS + h:ci * DN_HEADS + h + 1, :]
            s_scr[h] = s * egl + dot(kd_t, vb)
        if pending:
            out_finish(pending[0], prev_outs)
        pending = (rows, ci % 2, ws, vbs)
    out_finish(pending[0], out_matmuls(*pending))


def _deltanet(proj, gates, conv_w, norm_w, layer, batch, seq):
    m = proj.shape[0]
    tb = DN_BLOCK
    nblk = m // tb
    masks = jnp.asarray(_dn_masks())
    prev_per_blk = tb // DN_PREV_ROWS
    cols = (COL_Q // DN_WIDTH, COL_K // DN_WIDTH, COL_V // DN_WIDTH)
    cur = [pl.BlockSpec((tb, DN_WIDTH), lambda i, cb=cb: (i, cb)) for cb in cols]
    prev = [pl.BlockSpec((DN_PREV_ROWS, DN_WIDTH),
                         lambda i, cb=cb: (jnp.maximum(i * prev_per_blk - 1, 0), cb)) for cb in cols]
    tok = pl.BlockSpec((tb, DN_WIDTH), lambda i: (i, 0))
    egl_rows = (tb // DN_CHUNK) * DN_HEADS
    bf_out = jax.ShapeDtypeStruct((m, DN_WIDTH), BF16)
    u, w, qd, kd, qk, egl = pl.pallas_call(
        functools.partial(_dn_intra_kernel, blocks_per_seq=seq // tb),
        grid=(nblk,),
        in_specs=cur + prev + [
            pl.BlockSpec((tb, LANES), lambda i: (i, 0)),
            pl.BlockSpec((None, DN_CONV, 3 * DN_WIDTH), lambda i: (layer, 0, 0)),
            pl.BlockSpec(masks.shape, lambda i: (0, 0, 0))],
        out_specs=[tok, tok, tok, tok, tok, pl.BlockSpec((egl_rows, LANES), lambda i: (i, 0))],
        out_shape=[jax.ShapeDtypeStruct((m, DN_WIDTH), F32), bf_out, bf_out, bf_out, bf_out,
                   jax.ShapeDtypeStruct((m // DN_CHUNK * DN_HEADS, LANES), F32)],
        scratch_shapes=[pltpu.VMEM((3, tb + DN_PREV_ROWS, DN_WIDTH), F32),
                        pltpu.VMEM((3, tb, DN_WIDTH), F32)],
        compiler_params=_params("parallel"),
        name="dn_intra",
    )(*([proj] * 6), gates, conv_w, masks)

    ts = 256
    nt = seq // ts
    blk = lambda cb: pl.BlockSpec((ts, DN_WIDTH), lambda b, t: (b * nt + t, cb))
    return pl.pallas_call(
        functools.partial(_dn_scan_kernel, tb=ts),
        grid=(batch, nt),
        in_specs=[blk(0), blk(0), blk(0), blk(0), blk(0), blk(COL_Z // DN_WIDTH),
                  pl.BlockSpec((ts // DN_CHUNK * DN_HEADS, LANES), lambda b, t: (b * nt + t, 0)),
                  pl.BlockSpec((None, 1, DN_HEAD_DIM), lambda b, t: (layer, 0, 0))],
        out_specs=blk(0),
        out_shape=bf_out,
        scratch_shapes=[pltpu.VMEM((DN_HEADS, DN_HEAD_DIM, DN_HEAD_DIM), F32)],
        compiler_params=_params("arbitrary", "arbitrary"),
        name="dn_scan",
    )(u, w, qd, kd, qk, proj, egl, norm_w)


SWA_TILE = 2 * LANES


def _swa_consts():
    r = np.arange(SWA_TILE)[:, None]
    c = np.arange(SWA_TILE)[None, :]
    ones = (r // SWA_HEAD_DIM) == (c // SWA_HEAD_DIM)
    cl = c % SWA_HEAD_DIM
    perm = ((cl < ROPE_HALF) & (r == c + ROPE_HALF)) | ((cl >= ROPE_HALF) & (cl < ROPE_DIM) & (r == c - ROPE_HALF))
    return np.stack([ones, perm]).astype(np.float32)


def _norm_rope_tiles(tiles, nws, cos_t, sin_t, const_ref):
    dot = functools.partial(jnp.dot, preferred_element_type=F32)
    ws = [t.shape[1] for t in tiles]
    ms = [dot((t * t).astype(BF16), const_ref[0, :w, :w]) for t, w in zip(tiles, ws)]
    ys = [t * lax.rsqrt(m * (1.0 / SWA_HEAD_DIM) + EPS) * nw for t, m, nw in zip(tiles, ms, nws)]
    partners = [dot(y.astype(BF16), const_ref[1, :w, :w]) for y, w in zip(ys, ws)]
    return [y * cos_t[:, :w] + p * sin_t[:, :w] for y, p, w in zip(ys, partners, ws)]


def _swa_kernel(sink_ref, q_ref, k_ref, v_ref, cos_ref, sin_ref, qn_ref, kn_ref, const_ref, o_ref,
                kprev, vprev):
    n = pl.program_id(1)
    blk = SWA_BLOCK

    @pl.when(n == 0)
    def _():
        kprev[...] = jnp.zeros_like(kprev)
        vprev[...] = jnp.zeros_like(vprev)

    cos_t = jnp.tile(cos_ref[...], (1, SWA_TILE // LANES))
    sin_t = jnp.tile(sin_ref[...], (1, SWA_TILE // LANES))
    n_qt = SWA_WIDTH // SWA_TILE
    tiles = [q_ref[:, t * SWA_TILE:(t + 1) * SWA_TILE].astype(F32) for t in range(n_qt)] + [k_ref[...].astype(F32)]
    nws = [qn_ref[:, t * SWA_TILE:(t + 1) * SWA_TILE] for t in range(n_qt)] + [kn_ref[...]]
    roped = _norm_rope_tiles(tiles, nws, cos_t, sin_t, const_ref)
    qb = jnp.concatenate([(t * (SWA_HEAD_DIM ** -0.5 * LOG2E)).astype(BF16) for t in roped[:n_qt]], axis=1)

    low = lax.broadcasted_iota(jnp.int32, (blk, LANES), 1) < SWA_HEAD_DIM

    def split_heads(cur, prev_ref):
        swap = pltpu.roll(cur, SWA_HEAD_DIM, axis=1)
        parts = [jnp.where(low, cur, 0.0), jnp.where(low, swap, 0.0),
                 jnp.where(low, 0.0, swap), jnp.where(low, 0.0, cur)]
        bands = []
        for idx, part in enumerate(parts):
            part = part.astype(BF16)
            bands.append(jnp.concatenate([prev_ref[idx], part], axis=0))
            prev_ref[idx] = part
        return bands

    k_bands = split_heads(roped[n_qt], kprev)
    v_bands = split_heads(v_ref[...].astype(F32), vprev)

    qi = lax.broadcasted_iota(jnp.int32, (blk, 2 * blk), 0)
    kj = lax.broadcasted_iota(jnp.int32, (blk, 2 * blk), 1)
    rel = qi + blk - kj
    bias = jnp.where(rel >= 0, jnp.where(rel < blk, 0.0, NEG_BIG), NEG_BIG)
    prev_penalty = jnp.where(n > 0, 0.0, NEG_BIG)
    bias = bias + jnp.where(kj < blk, prev_penalty, 0.0)

    group = SWA_Q_HEADS // SWA_KV_HEADS
    for h0 in range(0, SWA_Q_HEADS, SWA_HEAD_BATCH):
        heads = range(h0, h0 + SWA_HEAD_BATCH)
        kops = [k_bands[2 * (h % 2) + h // group] for h in heads]
        vops = [v_bands[2 * (h % 2) + h // group] for h in heads]
        sinks = [sink_ref[h] * LOG2E for h in heads]
        ss = [_mm_nt(qb[:, (h // 2) * LANES:(h // 2 + 1) * LANES], kop) + bias
              for h, kop in zip(heads, kops)]
        mxs = [jnp.maximum(jnp.max(s, axis=-1, keepdims=True), sink) for s, sink in zip(ss, sinks)]
        ps = [jnp.exp2(s - mx) for s, mx in zip(ss, mxs)]
        denoms = [jnp.sum(p, axis=-1, keepdims=True) + jnp.exp2(sink - mx)
                  for p, sink, mx in zip(ps, sinks, mxs)]
        outs = [_mm(p, vop) * (1.0 / d) for p, vop, d in zip(ps, vops, denoms)]
        for pair in range(h0 // 2, (h0 + SWA_HEAD_BATCH) // 2):
            o_ref[:, pair * LANES:(pair + 1) * LANES] = (
                outs[2 * pair - h0] + outs[2 * pair + 1 - h0]).astype(BF16)


def _swa(proj, cos_t, sin_t, sinks, qn, kn, layer, batch, seq):
    m = proj.shape[0]
    blk = SWA_BLOCK
    nb = seq // blk
    row = lambda b, n: b * nb + n
    consts = jnp.asarray(_swa_consts(), BF16)
    return pl.pallas_call(
        _swa_kernel,
        grid=(batch, nb),
        in_specs=[pl.BlockSpec(memory_space=pltpu.SMEM),
                  pl.BlockSpec((blk, SWA_WIDTH), lambda b, n: (row(b, n), COL_SWQ // SWA_WIDTH)),
                  pl.BlockSpec((blk, SWA_KV_WIDTH), lambda b, n: (row(b, n), COL_SWK // SWA_KV_WIDTH)),
                  pl.BlockSpec((blk, SWA_KV_WIDTH), lambda b, n: (row(b, n), COL_SWV // SWA_KV_WIDTH)),
                  pl.BlockSpec((blk, LANES), lambda b, n: (row(b, n), 0)),
                  pl.BlockSpec((blk, LANES), lambda b, n: (row(b, n), 0)),
                  pl.BlockSpec((None, 1, SWA_WIDTH), lambda b, n: (layer, 0, 0)),
                  pl.BlockSpec((None, 1, SWA_KV_WIDTH), lambda b, n: (layer, 0, 0)),
                  pl.BlockSpec(consts.shape, lambda b, n: (0, 0, 0))],
        out_specs=pl.BlockSpec((blk, SWA_WIDTH), lambda b, n: (row(b, n), 0)),
        out_shape=jax.ShapeDtypeStruct((m, SWA_WIDTH), BF16),
        scratch_shapes=[pltpu.VMEM((2 * SWA_KV_HEADS, blk, SWA_KV_WIDTH), BF16),
                        pltpu.VMEM((2 * SWA_KV_HEADS, blk, SWA_KV_WIDTH), BF16)],
        compiler_params=_params("arbitrary", "arbitrary"),
        name="swa",
    )(sinks, proj, proj, proj, cos_t, sin_t, qn, kn, consts)


def _merge_kernel(x_ref, odn_ref, osw_ref, ga_ref, gb_ref, gt_ref, wdn_ref, wsw_ref, wo_ref, o_ref):
    ya = jnp.dot(odn_ref[...], wdn_ref[...], preferred_element_type=F32)
    yb = jnp.dot(osw_ref[...], wsw_ref[...], preferred_element_type=F32)
    merged = _sigmoid(ga_ref[...].astype(F32)) * ya + _sigmoid(gb_ref[...].astype(F32)) * yb
    out = jnp.dot(merged.astype(BF16), wo_ref[...], preferred_element_type=F32)
    o_ref[...] = x_ref[...] + gt_ref[0] * out


def _merge(x, o_dn, o_sw, proj, mod, w_dn, w_sw, w_o, layer, seq):
    m = x.shape[0]
    tm = 512
    tiles_per_seq = seq // tm
    tok = lambda cb: pl.BlockSpec((tm, D_MODEL), lambda i: (i, cb))
    wfull = pl.BlockSpec((None, D_MODEL, D_MODEL), lambda i: (layer, 0, 0))
    return pl.pallas_call(
        _merge_kernel,
        grid=(m // tm,),
        in_specs=[tok(0), tok(0), tok(0), tok(COL_GA // D_MODEL), tok(COL_GB // D_MODEL),
                  _mod_spec(layer, 2, tiles_per_seq), wfull, wfull, wfull],
        out_specs=tok(0),
        out_shape=jax.ShapeDtypeStruct((m, D_MODEL), F32),
        compiler_params=_params("arbitrary"),
        name="merge_out",
    )(x, o_dn, o_sw, proj, proj, mod, w_dn, w_sw, w_o)


def _ffn_kernel(x_ref, nw_ref, sc_ref, sh_ref, gt_ref, wa_ref, wl_ref, cw_ref, cb_ref, wd_ref,
                o_ref, h_scr, acc_scr, abuf, halo_scr, *, tm, tiles_per_seq, n_ff_tiles):
    i = pl.program_id(0)
    j = pl.program_id(1)
    halo = SUBLANES

    @pl.when(j == 0)
    def _():
        h_scr[...] = _norm_mod(x_ref[...], nw_ref[...], sc_ref[0], sh_ref[0]).astype(BF16)

    h = h_scr[...]
    a = jnp.dot(h, wa_ref[...], preferred_element_type=F32)
    lin = jnp.dot(h, wl_ref[...], preferred_element_type=F32)

    first = (i % tiles_per_seq) == 0
    prev = halo_scr[j]
    abuf[0:halo, :] = jnp.where(first, jnp.zeros_like(prev), prev)
    abuf[halo:halo + tm, :] = a
    halo_scr[j] = a[tm - halo:tm, :]
    w = cw_ref[...]
    y = w[FFN_CONV - 1:FFN_CONV, :] * a + cb_ref[...]
    for s in range(1, FFN_CONV):
        y = y + w[FFN_CONV - 1 - s:FFN_CONV - s, :] * abuf[halo - s:halo - s + tm, :]
    act = (_silu(y) * lin).astype(BF16)
    part = jnp.dot(act, wd_ref[...], preferred_element_type=F32)

    @pl.when(j == 0)
    def _():
        acc_scr[...] = part

    @pl.when(j > 0)
    def _():
        acc_scr[...] += part

    @pl.when(j == n_ff_tiles - 1)
    def _():
        o_ref[...] = x_ref[...] + gt_ref[0] * acc_scr[...]


def _ffn(x, nw, mod, w_up, conv_w, conv_b, w_down, layer, seq):
    m = x.shape[0]
    tm = 512
    n_ff = 2
    fc = D_FF // n_ff
    tiles_per_seq = seq // tm
    kern = functools.partial(_ffn_kernel, tm=tm, tiles_per_seq=tiles_per_seq, n_ff_tiles=n_ff)
    return pl.pallas_call(
        kern,
        grid=(m // tm, n_ff),
        in_specs=[pl.BlockSpec((tm, D_MODEL), lambda i, j: (i, 0)),
                  pl.BlockSpec((None, 1, D_MODEL), lambda i, j: (layer, 0, 0)),
                  _mod_spec(layer, 4, tiles_per_seq), _mod_spec(layer, 3, tiles_per_seq),
                  _mod_spec(layer, 5, tiles_per_seq),
                  pl.BlockSpec((None, D_MODEL, fc), lambda i, j: (layer, 0, j)),
                  pl.BlockSpec((None, D_MODEL, fc), lambda i, j: (layer, 0, n_ff + j)),
                  pl.BlockSpec((None, FFN_CONV, fc), lambda i, j: (layer, 0, j)),
                  pl.BlockSpec((None, 1, fc), lambda i, j: (layer, 0, j)),
                  pl.BlockSpec((None, fc, D_MODEL), lambda i, j: (layer, j, 0))],
        out_specs=pl.BlockSpec((tm, D_MODEL), lambda i, j: (i, 0)),
        out_shape=jax.ShapeDtypeStruct((m, D_MODEL), F32),
        scratch_shapes=[pltpu.VMEM((tm, D_MODEL), BF16),
                        pltpu.VMEM((tm, D_MODEL), F32),
                        pltpu.VMEM((tm + SUBLANES, fc), F32),
                        pltpu.VMEM((n_ff, SUBLANES, fc), F32)],
        compiler_params=_params("arbitrary", "arbitrary"),
        name="ffn",
    )(x, nw, mod, mod, mod, w_up, w_up, conv_w, conv_b, w_down)


def _pack_moves():
    o_a = 4 * DN_WIDTH
    o_swq = o_a + 2 * DN_HEADS
    o_swk = o_swq + SWA_WIDTH
    o_swv = o_swk + SWA_KV_WIDTH
    o_ga = o_swv + SWA_KV_WIDTH
    o_gb = o_ga + D_MODEL
    return ((0, COL_Q, o_a), (o_swq, COL_SWQ, SWA_WIDTH), (o_ga, COL_GA, D_MODEL), (o_gb, COL_GB, D_MODEL),
            (o_swk, COL_SWK, SWA_KV_WIDTH), (o_swv, COL_SWV, SWA_KV_WIDTH), (o_a, COL_AB, 2 * DN_HEADS))


def _pack_kernel(w_ref, o_ref):
    for src, dst, width in _pack_moves():
        o_ref[:, dst:dst + width] = w_ref[:, src:src + width].astype(BF16)
    tail = COL_AB + 2 * DN_HEADS
    o_ref[:, tail:] = jnp.zeros((o_ref.shape[0], IN_PACKED - tail), BF16)


def _pack_w_in(w_in):
    depth, d_in, n_in = w_in.shape
    tr = 256
    return pl.pallas_call(
        _pack_kernel,
        grid=(depth, d_in // tr),
        in_specs=[pl.BlockSpec((None, tr, n_in), lambda l, i: (l, i, 0))],
        out_specs=pl.BlockSpec((None, tr, IN_PACKED), lambda l, i: (l, i, 0)),
        out_shape=jax.ShapeDtypeStruct((depth, d_in, IN_PACKED), BF16),
        compiler_params=_params("parallel", "parallel"),
        name="pack_w_in",
    )(w_in)


def _lane_row(v):
    depth, n = v.shape
    return jnp.zeros((depth, 1, LANES), F32).at[:, 0, :n].set(v.astype(F32))


def kernel(x, c, positions, w_ada, b_ada, norm_mix, w_in, dn_conv, dn_a_log, dn_dt_bias, dn_norm,
           w_dn_out, swa_q_norm, swa_k_norm, swa_sinks, w_swa_out, w_o, norm_ffn, w_up, ffn_conv,
           ffn_conv_b, w_down):
    batch, seq, _ = x.shape
    depth = w_ada.shape[0]
    m = batch * seq

    mod_all = _ada_mod(c, w_ada, b_ada)
    cos_t, sin_t = _rope_tables(positions)

    w_in_p = _pack_w_in(w_in)
    w_dn_b, w_sw_b, w_o_b = w_dn_out.astype(BF16), w_swa_out.astype(BF16), w_o.astype(BF16)
    w_up_b, w_down_b = w_up.astype(BF16), w_down.astype(BF16)
    alog = _lane_row(dn_a_log)
    dtb = _lane_row(dn_dt_bias)
    qn = jnp.tile(swa_q_norm, (1, SWA_Q_HEADS)).reshape(depth, 1, SWA_WIDTH)
    kn = jnp.tile(swa_k_norm, (1, SWA_KV_HEADS)).reshape(depth, 1, SWA_KV_WIDTH)

    mod = mod_all.reshape(depth * SUBLANES * 6, 1, D_MODEL)
    norm_mix3 = norm_mix.reshape(depth, 1, D_MODEL)
    norm_ffn3 = norm_ffn.reshape(depth, 1, D_MODEL)
    dn_norm3 = dn_norm.reshape(depth, 1, DN_HEAD_DIM)
    conv_b3 = ffn_conv_b.reshape(depth, 1, D_FF)

    xf = x.reshape(m, D_MODEL)
    for l in range(depth):
        proj, gates = _inproj(xf, norm_mix3, mod, w_in_p, alog, dtb, l, seq)
        o_dn = _deltanet(proj, gates, dn_conv, dn_norm3, l, batch, seq)
        o_sw = _swa(proj, cos_t, sin_t, swa_sinks[l], qn, kn, l, batch, seq)
        xf = _merge(xf, o_dn, o_sw, proj, mod, w_dn_b, w_sw_b, w_o_b, l, seq)
        xf = _ffn(xf, norm_ffn3, mod, w_up_b, ffn_conv, conv_b3, w_down_b, l, seq)
    return xf.reshape(batch, seq, D_MODEL)
```

```python
import functools

import numpy as np
import jax
import jax.numpy as jnp
from jax import lax
from jax.experimental import pallas as pl
from jax.experimental.pallas import tpu as pltpu

F32 = jnp.float32
BF16 = jnp.bfloat16

D_MODEL = 1024
DN_HEADS = 8
DN_HEAD_DIM = 128
DN_WIDTH = DN_HEADS * DN_HEAD_DIM
DN_CONV = 4
DN_CHUNK = 64
SWA_Q_HEADS = 16
SWA_KV_HEADS = 2
SWA_HEAD_DIM = 64
SWA_WIDTH = SWA_Q_HEADS * SWA_HEAD_DIM
SWA_KV_WIDTH = SWA_KV_HEADS * SWA_HEAD_DIM
SWA_BLOCK = 128
SWA_STEP = 256
SWA_HEAD_BATCH = 16
ROPE_THETA = 500000.0
ROPE_DIM = SWA_HEAD_DIM // 4
ROPE_HALF = ROPE_DIM // 2
D_FF = 2816
FFN_CONV = 3
EPS = 1e-6

LANES = 128
SUBLANES = 8
VMEM_LIMIT = 56 * 1024 * 1024

COL_Q, COL_K, COL_V, COL_Z = 0, 1024, 2048, 3072
COL_SWQ, COL_GA, COL_GB = 4096, 5120, 6144
COL_SWK, COL_SWV, COL_AB = 7168, 7296, 7424
IN_PACKED = 7680
NEG_BIG = -1e30


def _sigmoid(x):
    return 1.0 / (1.0 + jnp.exp(-x))


def _silu(x):
    return x * _sigmoid(x)


def _mm(a, b):
    return jnp.dot(a.astype(BF16), b.astype(BF16), preferred_element_type=F32)


def _mm_nt(a, b):
    return lax.dot_general(a.astype(BF16), b.astype(BF16), (((1,), (1,)), ((), ())),
                           preferred_element_type=F32)


def _params(*sem):
    return pltpu.CompilerParams(dimension_semantics=sem, vmem_limit_bytes=VMEM_LIMIT)


def _ada_kernel(c_ref, w_ref, b_ref, o_ref):
    ca = _silu(c_ref[...])
    o_ref[0] = jnp.dot(ca, w_ref[0], precision=lax.Precision.HIGHEST,
                       preferred_element_type=F32) + b_ref[0]


def _ada_mod(c, w_ada, b_ada):
    depth = w_ada.shape[0]
    batch = c.shape[0]
    n_out = w_ada.shape[2]
    tn = 1536
    c_pad = jnp.zeros((SUBLANES, D_MODEL), F32).at[:batch].set(c)
    return pl.pallas_call(
        _ada_kernel,
        grid=(depth, n_out // tn),
        in_specs=[pl.BlockSpec((SUBLANES, D_MODEL), lambda l, j: (0, 0)),
                  pl.BlockSpec((1, D_MODEL, tn), lambda l, j: (l, 0, j)),
                  pl.BlockSpec((1, 1, tn), lambda l, j: (l, 0, j))],
        out_specs=pl.BlockSpec((1, SUBLANES, tn), lambda l, j: (l, 0, j)),
        out_shape=jax.ShapeDtypeStruct((depth, SUBLANES, n_out), F32),
        compiler_params=_params("arbitrary", "arbitrary"),
        name="ada_mod",
    )(c_pad, w_ada, b_ada.reshape(depth, 1, n_out))


def _rope_kernel(pos_ref, inv_ref, sgn_ref, cos_ref, sin_ref):
    ang = pos_ref[...] * inv_ref[...]
    on = sgn_ref[...] != 0.0
    cos_ref[...] = jnp.where(on, jnp.cos(ang), 1.0)
    sin_ref[...] = jnp.sin(ang) * sgn_ref[...]


def _rope_tables(positions):
    m = positions.size
    tm = min(2048, m)
    lane = np.arange(LANES) % SWA_HEAD_DIM
    inv = np.where(lane < ROPE_DIM,
                   np.power(ROPE_THETA, -(lane % ROPE_HALF).astype(np.float64) / ROPE_HALF), 0.0)
    sgn = np.where(lane < ROPE_HALF, -1.0, np.where(lane < ROPE_DIM, 1.0, 0.0))
    pos = positions.astype(F32).reshape(m, 1)
    return pl.pallas_call(
        _rope_kernel,
        grid=(m // tm,),
        in_specs=[pl.BlockSpec((tm, 1), lambda i: (i, 0)),
                  pl.BlockSpec((1, LANES), lambda i: (0, 0)),
                  pl.BlockSpec((1, LANES), lambda i: (0, 0))],
        out_specs=[pl.BlockSpec((tm, LANES), lambda i: (i, 0)),
                   pl.BlockSpec((tm, LANES), lambda i: (i, 0))],
        out_shape=[jax.ShapeDtypeStruct((m, LANES), F32)] * 2,
        compiler_params=_params("arbitrary"),
        name="rope_tables",
    )(pos, jnp.asarray(inv, F32).reshape(1, LANES), jnp.asarray(sgn, F32).reshape(1, LANES))


def _norm_mod(x, nw, sc, sh):
    ms = jnp.mean(x * x, axis=-1, keepdims=True)
    return (x * lax.rsqrt(ms + EPS) * nw) * (1.0 + sc) + sh


def _inproj_kernel(x_ref, nw_ref, sc_ref, sh_ref, w_ref, alog_ref, dtb_ref,
                   proj_ref, gate_ref, h_scr, *, n_col_tiles, ab_off):
    j = pl.program_id(1)

    @pl.when(j == 0)
    def _():
        h_scr[...] = _norm_mod(x_ref[...], nw_ref[...], sc_ref[0], sh_ref[0]).astype(BF16)

    acc = jnp.dot(h_scr[...], w_ref[...], preferred_element_type=F32)
    proj_ref[...] = acc.astype(BF16)

    @pl.when(j == n_col_tiles - 1)
    def _():
        ab = acc[:, ab_off:ab_off + LANES]
        z = ab + dtb_ref[...]
        softplus = jnp.maximum(z, 0.0) + jnp.log(1.0 + jnp.exp(-jnp.abs(z)))
        g = -jnp.exp(alog_ref[...]) * softplus
        lane = lax.broadcasted_iota(jnp.int32, ab.shape, 1)
        gate_ref[...] = jnp.where(lane < DN_HEADS, g, _sigmoid(ab))


def _mod_spec(layer, k, tiles_per_seq):
    return pl.BlockSpec((1, 1, D_MODEL),
                        lambda i, *_: ((layer * SUBLANES + i // tiles_per_seq) * 6 + k, 0, 0))


def _inproj(x, nw, mod, w, alog, dtb, layer, seq):
    m = x.shape[0]
    tm, tn = 1024, 2560
    nj = IN_PACKED // tn
    tiles_per_seq = seq // tm
    kern = functools.partial(_inproj_kernel, n_col_tiles=nj, ab_off=COL_AB - (nj - 1) * tn)
    return pl.pallas_call(
        kern,
        grid=(m // tm, nj),
        in_specs=[pl.BlockSpec((tm, D_MODEL), lambda i, j: (i, 0)),
                  pl.BlockSpec((None, 1, D_MODEL), lambda i, j: (layer, 0, 0)),
                  _mod_spec(layer, 1, tiles_per_seq), _mod_spec(layer, 0, tiles_per_seq),
                  pl.BlockSpec((None, D_MODEL, tn), lambda i, j: (layer, 0, j)),
                  pl.BlockSpec((None, 1, LANES), lambda i, j: (layer, 0, 0)),
                  pl.BlockSpec((None, 1, LANES), lambda i, j: (layer, 0, 0))],
        out_specs=[pl.BlockSpec((tm, tn), lambda i, j: (i, j)),
                   pl.BlockSpec((tm, LANES), lambda i, j: (i, 0))],
        out_shape=[jax.ShapeDtypeStruct((m, IN_PACKED), BF16),
                   jax.ShapeDtypeStruct((m, LANES), F32)],
        scratch_shapes=[pltpu.VMEM((tm, D_MODEL), BF16)],
        compiler_params=_params("arbitrary", "arbitrary"),
        name="inproj",
    )(x, nw, mod, mod, w, alog, dtb)


DN_BLOCK = 512
DN_PASS_UNITS = 2
DN_UNIT = 128
INV_BLOCK = 16
DN_PREV_ROWS = 16
M_INCL, M_STRICT, M_DIAG, M_EYE, M_NEG_INCL, M_OFF0 = 0, 1, 2, 3, 4, 5
LOG2E = float(np.log2(np.e))


def _dn_masks():
    r = np.arange(DN_UNIT)[:, None]
    c = np.arange(DN_UNIT)[None, :]
    same = lambda b: (r // b) == (c // b)
    chunk = same(DN_CHUNK)
    incl = chunk & (r >= c)
    masks = [incl, chunk & (r > c), same(INV_BLOCK), r == c, np.where(incl, 0.0, NEG_BIG)]
    b = INV_BLOCK
    while b < DN_CHUNK:
        masks.append(same(2 * b) & ~same(b))
        b *= 2
    return np.stack(masks).astype(np.float32)


def _inv_unit_lower(l_strict, mask_ref, filler):
    dot = functools.partial(jnp.dot, preferred_element_type=F32)
    ds = [l * mask_ref[M_DIAG] for l in l_strict]
    ts = [mask_ref[M_EYE] - d for d in ds]
    dbs = [d.astype(BF16) for d in ds]
    mpows = [dot(db, db) for db in dbs]
    filler()
    n_fac = int(np.log2(INV_BLOCK)) - 1
    for i in range(n_fac):
        mbs = [m.astype(BF16) for m in mpows]
        ts = [t + dot(t.astype(BF16), mb) for t, mb in zip(ts, mbs)]
        if i < n_fac - 1:
            mpows = [dot(mb, mb) for mb in mbs]
        filler()
    n_levels = int(np.log2(DN_CHUNK // INV_BLOCK))
    for lvl in range(n_levels):
        tbs = [t.astype(BF16) for t in ts]
        inner = [dot((l * mask_ref[M_OFF0 + lvl]).astype(BF16), tb).astype(BF16)
                 for l, tb in zip(l_strict, tbs)]
        filler()
        ts = [t - dot(tb, inn) for t, tb, inn in zip(ts, tbs, inner)]
        filler()
    return ts


def _dn_conv_pieces(raw_refs, prev_refs, keep_prev, cw_ref, xbuf, act_ref):
    tb, halo = DN_BLOCK, DN_PREV_ROWS

    def piece(idx, h):
        sl = slice(h * DN_HEAD_DIM, (h + 1) * DN_HEAD_DIM)
        ref, pref = raw_refs[idx], prev_refs[idx]
        if pref is None:
            xbuf[idx, 0:halo, sl] = jnp.zeros((halo, DN_HEAD_DIM), F32)
        else:
            xbuf[idx, 0:halo, sl] = pref[:, sl].astype(F32) * keep_prev
        x = ref[:, sl].astype(F32)
        xbuf[idx, halo:halo + tb, sl] = x
        w = cw_ref[:, idx * DN_WIDTH + h * DN_HEAD_DIM:idx * DN_WIDTH + (h + 1) * DN_HEAD_DIM]
        y = w[DN_CONV - 1:DN_CONV, :] * x
        for s in range(1, DN_CONV):
            y = y + w[DN_CONV - 1 - s:DN_CONV - s, :] * xbuf[idx, halo - s:halo - s + tb, sl]
        y = _silu(y)
        if idx < 2:
            scale = DN_HEAD_DIM ** -0.5 if idx == 0 else 1.0
            y = y * (lax.rsqrt(jnp.sum(y * y, axis=-1, keepdims=True) + EPS) * scale)
        act_ref[idx, :, sl] = y

    return [functools.partial(piece, idx, h) for idx in range(3) for h in range(DN_HEADS)]


def _dn_intra_kernel(q_ref, k_ref, v_ref, qp_ref, kp_ref, vp_ref, gate_ref, cw_ref, mask_ref,
                     u_ref, w_ref, qd_ref, kd_ref, qk_ref, egl_ref, xbuf, act_cur, *, blocks_per_seq):
    i = pl.program_id(0)
    tb, c = DN_BLOCK, DN_CHUNK
    n_chunks = tb // c

    keep_prev = jnp.where((i % blocks_per_seq) == 0, 0.0, 1.0)
    for piece in _dn_conv_pieces((q_ref, k_ref, v_ref), (qp_ref, kp_ref, vp_ref), keep_prev,
                                 cw_ref, xbuf, act_cur):
        piece()
    pending = []

    un = DN_UNIT
    units = [slice(p * un, (p + 1) * un) for p in range(tb // un)]
    gates = gate_ref[...]
    gcum = jnp.concatenate(
        [jnp.dot(mask_ref[M_INCL], gates[rows], precision=lax.Precision.HIGHEST, preferred_element_type=F32)
         for rows in units], axis=0)
    gcum_t = gcum.T
    gcum2 = gcum * LOG2E
    gcum2_t = gcum_t * LOG2E
    glast = jnp.concatenate(
        [jnp.broadcast_to(gcum[ci * c + c - 1:ci * c + c, :], (c, LANES)) for ci in range(n_chunks)], axis=0)
    e_cum = jnp.exp(gcum)
    e_rem = jnp.exp(glast - gcum)
    for ci in range(n_chunks):
        gl = gcum_t[0:DN_HEADS, ci * c + c - 1:ci * c + c]
        egl_ref[ci * DN_HEADS:(ci + 1) * DN_HEADS, :] = jnp.broadcast_to(jnp.exp(gl), (DN_HEADS, LANES))

    def head_setup(h, part, rhs, ls):
        sl = slice(h * DN_HEAD_DIM, (h + 1) * DN_HEAD_DIM)
        span = slice(part[0].start, part[-1].stop)
        qh, kh, vh = act_cur[0, span, sl], act_cur[1, span, sl], act_cur[2, span, sl]
        beta = jnp.broadcast_to(gates[span, DN_HEADS + h:DN_HEADS + h + 1], kh.shape)
        eg = jnp.broadcast_to(e_cum[span, h:h + 1], kh.shape)
        kb = kh * beta
        qd_ref[span, sl] = (qh * eg).astype(BF16)
        kd_ref[span, sl] = (kh * e_rem[span, h:h + 1]).astype(BF16)
        rhs_h = jnp.concatenate([vh * beta, kb * eg], axis=1).astype(BF16)
        for rows in part:
            loc = slice(rows.start - span.start, rows.stop - span.start)
            rhs.append(rhs_h[loc])
            a = _mm_nt(jnp.concatenate([kb[loc], qh[loc]], axis=0), kh[loc])
            decay = jnp.exp2(gcum2[rows, h:h + 1] - gcum2_t[h:h + 1, rows] + mask_ref[M_NEG_INCL])
            ls.append(a[:un] * (decay * mask_ref[M_STRICT]))
            qk_ref[rows, sl] = (a[un:] * decay).astype(BF16)

    def filler():
        if pending:
            pending.pop(0)()

    heads = range(DN_HEADS)
    for u0 in range(0, len(units), DN_PASS_UNITS):
        part = units[u0:u0 + DN_PASS_UNITS]
        rhs, ls = [], []
        for h in heads:
            head_setup(h, part, rhs, ls)
        tinvs = _inv_unit_lower(ls, mask_ref, filler)
        where = [(h, rows) for h in heads for rows in part]
        for (h, rows), tinv, r in zip(where, tinvs, rhs):
            sl = slice(h * DN_HEAD_DIM, (h + 1) * DN_HEAD_DIM)
            uw = jnp.dot(tinv.astype(BF16), r, preferred_element_type=F32)
            u_ref[rows, sl] = uw[:, :DN_HEAD_DIM]
            w_ref[rows, sl] = uw[:, DN_HEAD_DIM:].astype(BF16)


def _dn_scan_kernel(u_ref, w_ref, qd_ref, kd_ref, qk_ref, z_ref, egl_ref, nw_ref, o_ref, s_scr, *, tb):
    t = pl.program_id(1)
    c = DN_CHUNK

    @pl.when(t == 0)
    def _():
        s_scr[...] = jnp.zeros_like(s_scr)

    nw = nw_ref[...]
    zeros_v = jnp.zeros((c, DN_HEAD_DIM), BF16)
    heads = range(DN_HEADS)
    sls = [slice(h * DN_HEAD_DIM, (h + 1) * DN_HEAD_DIM) for h in heads]
    dot = functools.partial(jnp.dot, preferred_element_type=F32)

    def out_matmuls(rows, par, ws, vbs):
        pads = [jnp.concatenate([vb, zeros_v] if par == 0 else [zeros_v, vb], axis=0) for vb in vbs]
        return [w_s[c:] + dot(qk_ref[rows, sl], v_pad) for sl, w_s, v_pad in zip(sls, ws, pads)]

    def out_finish(rows, outs):
        for sl, o in zip(sls, outs):
            o = o * lax.rsqrt(jnp.mean(o * o, axis=-1, keepdims=True) + EPS) * nw
            o_ref[rows, sl] = (o * _silu(z_ref[rows, sl].astype(F32))).astype(BF16)

    pending = None
    for ci in range(tb // c):
        rows = slice(ci * c, (ci + 1) * c)
        s_old = [s_scr[h] for h in heads]
        ws = [dot(jnp.concatenate([w_ref[rows, sl], qd_ref[rows, sl]], axis=0), s.astype(BF16))
              for sl, s in zip(sls, s_old)]
        prev_outs = out_matmuls(*pending) if pending else None
        kd_ts = [kd_ref[rows, sl].astype(F32).T.astype(BF16) for sl in sls]
        vbs = [(u_ref[rows, sl] - w_s[:c]).astype(BF16) for sl, w_s in zip(sls, ws)]
        for h, s, vb, kd_t in zip(heads, s_old, vbs, kd_ts):
            egl = egl_ref[ci * DN_HEADS + h:ci * DN_HEADS + h + 1, :]
            s_scr[h] = s * egl + dot(kd_t, vb)
        if pending:
            out_finish(pending[0], prev_outs)
        pending = (rows, ci % 2, ws, vbs)
    out_finish(pending[0], out_matmuls(*pending))


def _deltanet(proj, gates, conv_w, norm_w, layer, batch, seq):
    m = proj.shape[0]
    tb = DN_BLOCK
    nblk = m // tb
    masks = jnp.asarray(_dn_masks())
    prev_per_blk = tb // DN_PREV_ROWS
    cols = (COL_Q // DN_WIDTH, COL_K // DN_WIDTH, COL_V // DN_WIDTH)
    cur = [pl.BlockSpec((tb, DN_WIDTH), lambda i, cb=cb: (i, cb)) for cb in cols]
    prev = [pl.BlockSpec((DN_PREV_ROWS, DN_WIDTH),
                         lambda i, cb=cb: (jnp.maximum(i * prev_per_blk - 1, 0), cb)) for cb in cols]
    tok = pl.BlockSpec((tb, DN_WIDTH), lambda i: (i, 0))
    egl_rows = (tb // DN_CHUNK) * DN_HEADS
    bf_out = jax.ShapeDtypeStruct((m, DN_WIDTH), BF16)
    u, w, qd, kd, qk, egl = pl.pallas_call(
        functools.partial(_dn_intra_kernel, blocks_per_seq=seq // tb),
        grid=(nblk,),
        in_specs=cur + prev + [
            pl.BlockSpec((tb, LANES), lambda i: (i, 0)),
            pl.BlockSpec((None, DN_CONV, 3 * DN_WIDTH), lambda i: (layer, 0, 0)),
            pl.BlockSpec(masks.shape, lambda i: (0, 0, 0))],
        out_specs=[tok, tok, tok, tok, tok, pl.BlockSpec((egl_rows, LANES), lambda i: (i, 0))],
        out_shape=[jax.ShapeDtypeStruct((m, DN_WIDTH), F32), bf_out, bf_out, bf_out, bf_out,
                   jax.ShapeDtypeStruct((m // DN_CHUNK * DN_HEADS, LANES), F32)],
        scratch_shapes=[pltpu.VMEM((3, tb + DN_PREV_ROWS, DN_WIDTH), F32),
                        pltpu.VMEM((3, tb, DN_WIDTH), F32)],
        compiler_params=_params("parallel"),
        name="dn_intra",
    )(*([proj] * 6), gates, conv_w, masks)

    ts = 512
    nt = seq // ts
    blk = lambda cb: pl.BlockSpec((ts, DN_WIDTH), lambda b, t: (b * nt + t, cb))
    return pl.pallas_call(
        functools.partial(_dn_scan_kernel, tb=ts),
        grid=(batch, nt),
        in_specs=[blk(0), blk(0), blk(0), blk(0), blk(0), blk(COL_Z // DN_WIDTH),
                  pl.BlockSpec((ts // DN_CHUNK * DN_HEADS, LANES), lambda b, t: (b * nt + t, 0)),
                  pl.BlockSpec((None, 1, DN_HEAD_DIM), lambda b, t: (layer, 0, 0))],
        out_specs=blk(0),
        out_shape=bf_out,
        scratch_shapes=[pltpu.VMEM((DN_HEADS, DN_HEAD_DIM, DN_HEAD_DIM), F32)],
        compiler_params=_params("arbitrary", "arbitrary"),
        name="dn_scan",
    )(u, w, qd, kd, qk, proj, egl, norm_w)


SWA_TILE = 2 * LANES


def _swa_consts():
    r = np.arange(SWA_TILE)[:, None]
    c = np.arange(SWA_TILE)[None, :]
    ones = (r // SWA_HEAD_DIM) == (c // SWA_HEAD_DIM)
    cl = c % SWA_HEAD_DIM
    perm = ((cl < ROPE_HALF) & (r == c + ROPE_HALF)) | ((cl >= ROPE_HALF) & (cl < ROPE_DIM) & (r == c - ROPE_HALF))
    return np.stack([ones, perm]).astype(np.float32)


def _norm_rope_tiles(tiles, nws, cos_t, sin_t, const_ref):
    dot = functools.partial(jnp.dot, preferred_element_type=F32)
    ws = [t.shape[1] for t in tiles]
    ms = [dot((t * t).astype(BF16), const_ref[0, :w, :w]) for t, w in zip(tiles, ws)]
    ys = [t * lax.rsqrt(m * (1.0 / SWA_HEAD_DIM) + EPS) * nw for t, m, nw in zip(tiles, ms, nws)]
    partners = [dot(y.astype(BF16), const_ref[1, :w, :w]) for y, w in zip(ys, ws)]
    return [y * cos_t[:, :w] + p * sin_t[:, :w] for y, p, w in zip(ys, partners, ws)]


def _swa_kernel(sink_ref, q_ref, k_ref, v_ref, cos_ref, sin_ref, qn_ref, kn_ref, const_ref, o_ref,
                kprev, vprev):
    n = pl.program_id(1)
    blk = SWA_BLOCK

    @pl.when(n == 0)
    def _():
        kprev[...] = jnp.zeros_like(kprev)
        vprev[...] = jnp.zeros_like(vprev)

    for sb in range(SWA_STEP // blk):
        rows = slice(sb * blk, (sb + 1) * blk)
        prev_penalty = jnp.where(n > 0, 0.0, NEG_BIG) if sb == 0 else 0.0
        _swa_block(rows, prev_penalty, sink_ref, q_ref, k_ref, v_ref, cos_ref, sin_ref, qn_ref, kn_ref,
                   const_ref, o_ref, kprev, vprev)


def _swa_block(rows, prev_penalty, sink_ref, q_ref, k_ref, v_ref, cos_ref, sin_ref, qn_ref, kn_ref,
               const_ref, o_ref, kprev, vprev):
    blk = SWA_BLOCK
    cos_t = jnp.tile(cos_ref[rows, :], (1, SWA_TILE // LANES))
    sin_t = jnp.tile(sin_ref[rows, :], (1, SWA_TILE // LANES))
    n_qt = SWA_WIDTH // SWA_TILE
    tiles = ([q_ref[rows, t * SWA_TILE:(t + 1) * SWA_TILE].astype(F32) for t in range(n_qt)]
             + [k_ref[rows, :].astype(F32)])
    nws = [qn_ref[:, t * SWA_TILE:(t + 1) * SWA_TILE] for t in range(n_qt)] + [kn_ref[...]]
    roped = _norm_rope_tiles(tiles, nws, cos_t, sin_t, const_ref)
    qb = jnp.concatenate([(t * (SWA_HEAD_DIM ** -0.5 * LOG2E)).astype(BF16) for t in roped[:n_qt]], axis=1)

    low = lax.broadcasted_iota(jnp.int32, (blk, LANES), 1) < SWA_HEAD_DIM

    def split_heads(cur, prev_ref):
        swap = pltpu.roll(cur, SWA_HEAD_DIM, axis=1)
        parts = [jnp.where(low, cur, 0.0), jnp.where(low, swap, 0.0),
                 jnp.where(low, 0.0, swap), jnp.where(low, 0.0, cur)]
        bands = []
        for idx, part in enumerate(parts):
            part = part.astype(BF16)
            bands.append(jnp.concatenate([prev_ref[idx], part], axis=0))
            prev_ref[idx] = part
        return bands

    k_bands = split_heads(roped[n_qt], kprev)
    v_bands = split_heads(v_ref[rows, :].astype(F32), vprev)

    qi = lax.broadcasted_iota(jnp.int32, (blk, 2 * blk), 0)
    kj = lax.broadcasted_iota(jnp.int32, (blk, 2 * blk), 1)
    rel = qi + blk - kj
    bias = jnp.where(rel >= 0, jnp.where(rel < blk, 0.0, NEG_BIG), NEG_BIG)
    bias = bias + jnp.where(kj < blk, prev_penalty, 0.0)

    group = SWA_Q_HEADS // SWA_KV_HEADS
    for h0 in range(0, SWA_Q_HEADS, SWA_HEAD_BATCH):
        heads = range(h0, h0 + SWA_HEAD_BATCH)
        kops = [k_bands[2 * (h % 2) + h // group] for h in heads]
        vops = [v_bands[2 * (h % 2) + h // group] for h in heads]
        sinks = [sink_ref[h] * LOG2E for h in heads]
        ss = [_mm_nt(qb[:, (h // 2) * LANES:(h // 2 + 1) * LANES], kop) + bias
              for h, kop in zip(heads, kops)]
        mxs = [jnp.maximum(jnp.max(s, axis=-1, keepdims=True), sink) for s, sink in zip(ss, sinks)]
        ps = [jnp.exp2(s - mx) for s, mx in zip(ss, mxs)]
        denoms = [jnp.sum(p, axis=-1, keepdims=True) + jnp.exp2(sink - mx)
                  for p, sink, mx in zip(ps, sinks, mxs)]
        outs = [_mm(p, vop) * (1.0 / d) for p, vop, d in zip(ps, vops, denoms)]
        for pair in range(h0 // 2, (h0 + SWA_HEAD_BATCH) // 2):
            o_ref[rows, pair * LANES:(pair + 1) * LANES] = (
                outs[2 * pair - h0] + outs[2 * pair + 1 - h0]).astype(BF16)


def _swa(proj, cos_t, sin_t, sinks, qn, kn, layer, batch, seq):
    m = proj.shape[0]
    blk, step = SWA_BLOCK, SWA_STEP
    nb = seq // step
    row = lambda b, n: b * nb + n
    consts = jnp.asarray(_swa_consts(), BF16)
    return pl.pallas_call(
        _swa_kernel,
        grid=(batch, nb),
        in_specs=[pl.BlockSpec(memory_space=pltpu.SMEM),
                  pl.BlockSpec((step, SWA_WIDTH), lambda b, n: (row(b, n), COL_SWQ // SWA_WIDTH)),
                  pl.BlockSpec((step, SWA_KV_WIDTH), lambda b, n: (row(b, n), COL_SWK // SWA_KV_WIDTH)),
                  pl.BlockSpec((step, SWA_KV_WIDTH), lambda b, n: (row(b, n), COL_SWV // SWA_KV_WIDTH)),
                  pl.BlockSpec((step, LANES), lambda b, n: (row(b, n), 0)),
                  pl.BlockSpec((step, LANES), lambda b, n: (row(b, n), 0)),
                  pl.BlockSpec((None, 1, SWA_WIDTH), lambda b, n: (layer, 0, 0)),
                  pl.BlockSpec((None, 1, SWA_KV_WIDTH), lambda b, n: (layer, 0, 0)),
                  pl.BlockSpec(consts.shape, lambda b, n: (0, 0, 0))],
        out_specs=pl.BlockSpec((step, SWA_WIDTH), lambda b, n: (row(b, n), 0)),
        out_shape=jax.ShapeDtypeStruct((m, SWA_WIDTH), BF16),
        scratch_shapes=[pltpu.VMEM((2 * SWA_KV_HEADS, blk, SWA_KV_WIDTH), BF16),
                        pltpu.VMEM((2 * SWA_KV_HEADS, blk, SWA_KV_WIDTH), BF16)],
        compiler_params=_params("arbitrary", "arbitrary"),
        name="swa",
    )(sinks, proj, proj, proj, cos_t, sin_t, qn, kn, consts)


def _merge_kernel(x_ref, odn_ref, osw_ref, ga_ref, gb_ref, gt_ref, wdn_ref, wsw_ref, wo_ref, o_ref):
    ya = jnp.dot(odn_ref[...], wdn_ref[...], preferred_element_type=F32)
    yb = jnp.dot(osw_ref[...], wsw_ref[...], preferred_element_type=F32)
    merged = _sigmoid(ga_ref[...].astype(F32)) * ya + _sigmoid(gb_ref[...].astype(F32)) * yb
    out = jnp.dot(merged.astype(BF16), wo_ref[...], preferred_element_type=F32)
    o_ref[...] = x_ref[...] + gt_ref[0] * out


def _merge(x, o_dn, o_sw, proj, mod, w_dn, w_sw, w_o, layer, seq):
    m = x.shape[0]
    tm = 1024
    tiles_per_seq = seq // tm
    tok = lambda cb: pl.BlockSpec((tm, D_MODEL), lambda i: (i, cb))
    wfull = pl.BlockSpec((None, D_MODEL, D_MODEL), lambda i: (layer, 0, 0))
    return pl.pallas_call(
        _merge_kernel,
        grid=(m // tm,),
        in_specs=[tok(0), tok(0), tok(0), tok(COL_GA // D_MODEL), tok(COL_GB // D_MODEL),
                  _mod_spec(layer, 2, tiles_per_seq), wfull, wfull, wfull],
        out_specs=tok(0),
        out_shape=jax.ShapeDtypeStruct((m, D_MODEL), F32),
        compiler_params=_params("arbitrary"),
        name="merge_out",
    )(x, o_dn, o_sw, proj, proj, mod, w_dn, w_sw, w_o)


def _ffn_kernel(x_ref, nw_ref, sc_ref, sh_ref, gt_ref, wa_ref, wl_ref, cw_ref, cb_ref, wd_ref,
                o_ref, h_scr, acc_scr, abuf, halo_scr, *, tm, tiles_per_seq, n_ff_tiles):
    i = pl.program_id(0)
    j = pl.program_id(1)
    halo = SUBLANES

    @pl.when(j == 0)
    def _():
        h_scr[...] = _norm_mod(x_ref[...], nw_ref[...], sc_ref[0], sh_ref[0]).astype(BF16)

    h = h_scr[...]
    a = jnp.dot(h, wa_ref[...], preferred_element_type=F32)
    lin = jnp.dot(h, wl_ref[...], preferred_element_type=F32)

    first = (i % tiles_per_seq) == 0
    prev = halo_scr[j]
    abuf[0:halo, :] = jnp.where(first, jnp.zeros_like(prev), prev)
    abuf[halo:halo + tm, :] = a
    halo_scr[j] = a[tm - halo:tm, :]
    w = cw_ref[...]
    y = w[FFN_CONV - 1:FFN_CONV, :] * a + cb_ref[...]
    for s in range(1, FFN_CONV):
        y = y + w[FFN_CONV - 1 - s:FFN_CONV - s, :] * abuf[halo - s:halo - s + tm, :]
    act = (_silu(y) * lin).astype(BF16)
    part = jnp.dot(act, wd_ref[...], preferred_element_type=F32)

    @pl.when(j == 0)
    def _():
        acc_scr[...] = part

    @pl.when(j > 0)
    def _():
        acc_scr[...] += part

    @pl.when(j == n_ff_tiles - 1)
    def _():
        o_ref[...] = x_ref[...] + gt_ref[0] * acc_scr[...]


def _ffn(x, nw, mod, w_up, conv_w, conv_b, w_down, layer, seq):
    m = x.shape[0]
    tm = 512
    n_ff = 2
    fc = D_FF // n_ff
    tiles_per_seq = seq // tm
    kern = functools.partial(_ffn_kernel, tm=tm, tiles_per_seq=tiles_per_seq, n_ff_tiles=n_ff)
    return pl.pallas_call(
        kern,
        grid=(m // tm, n_ff),
        in_specs=[pl.BlockSpec((tm, D_MODEL), lambda i, j: (i, 0)),
                  pl.BlockSpec((None, 1, D_MODEL), lambda i, j: (layer, 0, 0)),
                  _mod_spec(layer, 4, tiles_per_seq), _mod_spec(layer, 3, tiles_per_seq),
                  _mod_spec(layer, 5, tiles_per_seq),
                  pl.BlockSpec((None, D_MODEL, fc), lambda i, j: (layer, 0, j)),
                  pl.BlockSpec((None, D_MODEL, fc), lambda i, j: (layer, 0, n_ff + j)),
                  pl.BlockSpec((None, FFN_CONV, fc), lambda i, j: (layer, 0, j)),
                  pl.BlockSpec((None, 1, fc), lambda i, j: (layer, 0, j)),
                  pl.BlockSpec((None, fc, D_MODEL), lambda i, j: (layer, j, 0))],
        out_specs=pl.BlockSpec((tm, D_MODEL), lambda i, j: (i, 0)),
        out_shape=jax.ShapeDtypeStruct((m, D_MODEL), F32),
        scratch_shapes=[pltpu.VMEM((tm, D_MODEL), BF16),
                        pltpu.VMEM((tm, D_MODEL), F32),
                        pltpu.VMEM((tm + SUBLANES, fc), F32),
                        pltpu.VMEM((n_ff, SUBLANES, fc), F32)],
        compiler_params=_params("arbitrary", "arbitrary"),
        name="ffn",
    )(x, nw, mod, mod, mod, w_up, w_up, conv_w, conv_b, w_down)


def _pack_moves():
    o_a = 4 * DN_WIDTH
    o_swq = o_a + 2 * DN_HEADS
    o_swk = o_swq + SWA_WIDTH
    o_swv = o_swk + SWA_KV_WIDTH
    o_ga = o_swv + SWA_KV_WIDTH
    o_gb = o_ga + D_MODEL
    return ((0, COL_Q, o_a), (o_swq, COL_SWQ, SWA_WIDTH), (o_ga, COL_GA, D_MODEL), (o_gb, COL_GB, D_MODEL),
            (o_swk, COL_SWK, SWA_KV_WIDTH), (o_swv, COL_SWV, SWA_KV_WIDTH), (o_a, COL_AB, 2 * DN_HEADS))


def _pack_kernel(w_ref, o_ref):
    for src, dst, width in _pack_moves():
        o_ref[:, dst:dst + width] = w_ref[:, src:src + width].astype(BF16)
    tail = COL_AB + 2 * DN_HEADS
    o_ref[:, tail:] = jnp.zeros((o_ref.shape[0], IN_PACKED - tail), BF16)


def _pack_w_in(w_in):
    depth, d_in, n_in = w_in.shape
    tr = 256
    return pl.pallas_call(
        _pack_kernel,
        grid=(depth, d_in // tr),
        in_specs=[pl.BlockSpec((None, tr, n_in), lambda l, i: (l, i, 0))],
        out_specs=pl.BlockSpec((None, tr, IN_PACKED), lambda l, i: (l, i, 0)),
        out_shape=jax.ShapeDtypeStruct((depth, d_in, IN_PACKED), BF16),
        compiler_params=_params("parallel", "parallel"),
        name="pack_w_in",
    )(w_in)


def _lane_row(v):
    depth, n = v.shape
    return jnp.zeros((depth, 1, LANES), F32).at[:, 0, :n].set(v.astype(F32))


def kernel(x, c, positions, w_ada, b_ada, norm_mix, w_in, dn_conv, dn_a_log, dn_dt_bias, dn_norm,
           w_dn_out, swa_q_norm, swa_k_norm, swa_sinks, w_swa_out, w_o, norm_ffn, w_up, ffn_conv,
           ffn_conv_b, w_down):
    batch, seq, _ = x.shape
    depth = w_ada.shape[0]
    m = batch * seq

    mod_all = _ada_mod(c, w_ada, b_ada)
    cos_t, sin_t = _rope_tables(positions)

    w_in_p = _pack_w_in(w_in)
    w_dn_b, w_sw_b, w_o_b = w_dn_out.astype(BF16), w_swa_out.astype(BF16), w_o.astype(BF16)
    w_up_b, w_down_b = w_up.astype(BF16), w_down.astype(BF16)
    alog = _lane_row(dn_a_log)
    dtb = _lane_row(dn_dt_bias)
    qn = jnp.tile(swa_q_norm, (1, SWA_Q_HEADS)).reshape(depth, 1, SWA_WIDTH)
    kn = jnp.tile(swa_k_norm, (1, SWA_KV_HEADS)).reshape(depth, 1, SWA_KV_WIDTH)

    mod = mod_all.reshape(depth * SUBLANES * 6, 1, D_MODEL)
    norm_mix3 = norm_mix.reshape(depth, 1, D_MODEL)
    norm_ffn3 = norm_ffn.reshape(depth, 1, D_MODEL)
    dn_norm3 = dn_norm.reshape(depth, 1, DN_HEAD_DIM)
    conv_b3 = ffn_conv_b.reshape(depth, 1, D_FF)

    xf = x.reshape(m, D_MODEL)
    for l in range(depth):
        proj, gates = _inproj(xf, norm_mix3, mod, w_in_p, alog, dtb, l, seq)
        o_dn = _deltanet(proj, gates, dn_conv, dn_norm3, l, batch, seq)
        o_sw = _swa(proj, cos_t, sin_t, swa_sinks[l], qn, kn, l, batch, seq)
        xf = _merge(xf, o_dn, o_sw, proj, mod, w_dn_b, w_sw_b, w_o_b, l, seq)
        xf = _ffn(xf, norm_ffn3, mod, w_up_b, ffn_conv, conv_b3, w_down_b, l, seq)
    return xf.reshape(batch, seq, D_MODEL)
```

```python
import functools

import numpy as np
import jax
import jax.numpy as jnp
from jax import lax
from jax.experimental import pallas as pl
from jax.experimental.pallas import tpu as pltpu

F32 = jnp.float32
BF16 = jnp.bfloat16

D_MODEL = 1024
DN_HEADS = 8
DN_HEAD_DIM = 128
DN_WIDTH = DN_HEADS * DN_HEAD_DIM
DN_CONV = 4
DN_CHUNK = 64
SWA_Q_HEADS = 16
SWA_KV_HEADS = 2
SWA_HEAD_DIM = 64
SWA_WIDTH = SWA_Q_HEADS * SWA_HEAD_DIM
SWA_KV_WIDTH = SWA_KV_HEADS * SWA_HEAD_DIM
SWA_BLOCK = 128
SWA_STEP = 256
SWA_HEAD_BATCH = 16
ROPE_THETA = 500000.0
ROPE_DIM = SWA_HEAD_DIM // 4
ROPE_HALF = ROPE_DIM // 2
D_FF = 2816
FFN_CONV = 3
EPS = 1e-6

LANES = 128
SUBLANES = 8
VMEM_LIMIT = 56 * 1024 * 1024

COL_Q, COL_K, COL_V, COL_Z = 0, 1024, 2048, 3072
COL_SWQ, COL_GA, COL_GB = 4096, 5120, 6144
COL_SWK, COL_SWV, COL_AB = 7168, 7296, 7424
IN_PACKED = 7680
NEG_BIG = -1e30


def _sigmoid(x):
    return 1.0 / (1.0 + jnp.exp(-x))


def _silu(x):
    return x * _sigmoid(x)


def _mm(a, b):
    return jnp.dot(a.astype(BF16), b.astype(BF16), preferred_element_type=F32)


def _mm_nt(a, b):
    return lax.dot_general(a.astype(BF16), b.astype(BF16), (((1,), (1,)), ((), ())),
                           preferred_element_type=F32)


def _params(*sem):
    return pltpu.CompilerParams(dimension_semantics=sem, vmem_limit_bytes=VMEM_LIMIT)


def _ada_kernel(c_ref, w_ref, b_ref, o_ref):
    ca = _silu(c_ref[...])
    o_ref[0] = jnp.dot(ca, w_ref[0], precision=lax.Precision.HIGHEST,
                       preferred_element_type=F32) + b_ref[0]


def _ada_mod(c, w_ada, b_ada):
    depth = w_ada.shape[0]
    batch = c.shape[0]
    n_out = w_ada.shape[2]
    tn = 1536
    c_pad = jnp.zeros((SUBLANES, D_MODEL), F32).at[:batch].set(c)
    return pl.pallas_call(
        _ada_kernel,
        grid=(depth, n_out // tn),
        in_specs=[pl.BlockSpec((SUBLANES, D_MODEL), lambda l, j: (0, 0)),
                  pl.BlockSpec((1, D_MODEL, tn), lambda l, j: (l, 0, j)),
                  pl.BlockSpec((1, 1, tn), lambda l, j: (l, 0, j))],
        out_specs=pl.BlockSpec((1, SUBLANES, tn), lambda l, j: (l, 0, j)),
        out_shape=jax.ShapeDtypeStruct((depth, SUBLANES, n_out), F32),
        compiler_params=_params("arbitrary", "arbitrary"),
        name="ada_mod",
    )(c_pad, w_ada, b_ada.reshape(depth, 1, n_out))


def _rope_kernel(pos_ref, inv_ref, sgn_ref, cos_ref, sin_ref):
    ang = pos_ref[...] * inv_ref[...]
    on = sgn_ref[...] != 0.0
    cos_ref[...] = jnp.where(on, jnp.cos(ang), 1.0)
    sin_ref[...] = jnp.sin(ang) * sgn_ref[...]


def _rope_tables(positions):
    m = positions.size
    tm = min(2048, m)
    lane = np.arange(LANES) % SWA_HEAD_DIM
    inv = np.where(lane < ROPE_DIM,
                   np.power(ROPE_THETA, -(lane % ROPE_HALF).astype(np.float64) / ROPE_HALF), 0.0)
    sgn = np.where(lane < ROPE_HALF, -1.0, np.where(lane < ROPE_DIM, 1.0, 0.0))
    pos = positions.astype(F32).reshape(m, 1)
    return pl.pallas_call(
        _rope_kernel,
        grid=(m // tm,),
        in_specs=[pl.BlockSpec((tm, 1), lambda i: (i, 0)),
                  pl.BlockSpec((1, LANES), lambda i: (0, 0)),
                  pl.BlockSpec((1, LANES), lambda i: (0, 0))],
        out_specs=[pl.BlockSpec((tm, LANES), lambda i: (i, 0)),
                   pl.BlockSpec((tm, LANES), lambda i: (i, 0))],
        out_shape=[jax.ShapeDtypeStruct((m, LANES), F32)] * 2,
        compiler_params=_params("arbitrary"),
        name="rope_tables",
    )(pos, jnp.asarray(inv, F32).reshape(1, LANES), jnp.asarray(sgn, F32).reshape(1, LANES))


def _norm_mod(x, nw, sc, sh):
    ms = jnp.mean(x * x, axis=-1, keepdims=True)
    return (x * lax.rsqrt(ms + EPS) * nw) * (1.0 + sc) + sh


def _inproj_kernel(x_ref, nw_ref, sc_ref, sh_ref, w_ref, alog_ref, dtb_ref,
                   proj_ref, gate_ref, h_scr, *, n_col_tiles, ab_off):
    j = pl.program_id(1)

    @pl.when(j == 0)
    def _():
        h_scr[...] = _norm_mod(x_ref[...], nw_ref[...], sc_ref[0], sh_ref[0]).astype(BF16)

    acc = _mm_nt(h_scr[...], w_ref[...])
    proj_ref[...] = acc.astype(BF16)

    @pl.when(j == n_col_tiles - 1)
    def _():
        ab = acc[:, ab_off:ab_off + LANES]
        z = ab + dtb_ref[...]
        softplus = jnp.maximum(z, 0.0) + jnp.log(1.0 + jnp.exp(-jnp.abs(z)))
        g = -jnp.exp(alog_ref[...]) * softplus
        lane = lax.broadcasted_iota(jnp.int32, ab.shape, 1)
        gate_ref[...] = jnp.where(lane < DN_HEADS, g, _sigmoid(ab))


def _mod_spec(layer, k, tiles_per_seq):
    return pl.BlockSpec((1, 1, D_MODEL),
                        lambda i, *_: ((layer * SUBLANES + i // tiles_per_seq) * 6 + k, 0, 0))


def _inproj(x, nw, mod, w, alog, dtb, layer, seq):
    m = x.shape[0]
    tm, tn = 1024, 2560
    nj = IN_PACKED // tn
    tiles_per_seq = seq // tm
    kern = functools.partial(_inproj_kernel, n_col_tiles=nj, ab_off=COL_AB - (nj - 1) * tn)
    return pl.pallas_call(
        kern,
        grid=(m // tm, nj),
        in_specs=[pl.BlockSpec((tm, D_MODEL), lambda i, j: (i, 0)),
                  pl.BlockSpec((None, 1, D_MODEL), lambda i, j: (layer, 0, 0)),
                  _mod_spec(layer, 1, tiles_per_seq), _mod_spec(layer, 0, tiles_per_seq),
                  pl.BlockSpec((None, tn, D_MODEL), lambda i, j: (layer, j, 0)),
                  pl.BlockSpec((None, 1, LANES), lambda i, j: (layer, 0, 0)),
                  pl.BlockSpec((None, 1, LANES), lambda i, j: (layer, 0, 0))],
        out_specs=[pl.BlockSpec((tm, tn), lambda i, j: (i, j)),
                   pl.BlockSpec((tm, LANES), lambda i, j: (i, 0))],
        out_shape=[jax.ShapeDtypeStruct((m, IN_PACKED), BF16),
                   jax.ShapeDtypeStruct((m, LANES), F32)],
        scratch_shapes=[pltpu.VMEM((tm, D_MODEL), BF16)],
        compiler_params=_params("arbitrary", "arbitrary"),
        name="inproj",
    )(x, nw, mod, mod, w, alog, dtb)


DN_BLOCK = 512
DN_PASS_UNITS = 2
DN_UNIT = 128
INV_BLOCK = 16
DN_PREV_ROWS = 16
M_INCL, M_STRICT, M_DIAG, M_EYE, M_NEG_INCL, M_OFF0 = 0, 1, 2, 3, 4, 5
LOG2E = float(np.log2(np.e))


def _dn_masks():
    r = np.arange(DN_UNIT)[:, None]
    c = np.arange(DN_UNIT)[None, :]
    same = lambda b: (r // b) == (c // b)
    chunk = same(DN_CHUNK)
    incl = chunk & (r >= c)
    masks = [incl, chunk & (r > c), same(INV_BLOCK), r == c, np.where(incl, 0.0, NEG_BIG)]
    b = INV_BLOCK
    while b < DN_CHUNK:
        masks.append(same(2 * b) & ~same(b))
        b *= 2
    return np.stack(masks).astype(np.float32)


def _inv_unit_lower(l_strict, mask_ref, filler):
    dot = functools.partial(jnp.dot, preferred_element_type=F32)
    ds = [l * mask_ref[M_DIAG] for l in l_strict]
    ts = [mask_ref[M_EYE] - d for d in ds]
    dbs = [d.astype(BF16) for d in ds]
    mpows = [dot(db, db) for db in dbs]
    filler()
    n_fac = int(np.log2(INV_BLOCK)) - 1
    for i in range(n_fac):
        mbs = [m.astype(BF16) for m in mpows]
        ts = [t + dot(t.astype(BF16), mb) for t, mb in zip(ts, mbs)]
        if i < n_fac - 1:
            mpows = [dot(mb, mb) for mb in mbs]
        filler()
    n_levels = int(np.log2(DN_CHUNK // INV_BLOCK))
    for lvl in range(n_levels):
        tbs = [t.astype(BF16) for t in ts]
        inner = [dot((l * mask_ref[M_OFF0 + lvl]).astype(BF16), tb).astype(BF16)
                 for l, tb in zip(l_strict, tbs)]
        filler()
        ts = [t - dot(tb, inn) for t, tb, inn in zip(ts, tbs, inner)]
        filler()
    return ts


def _dn_conv_pieces(raw_refs, prev_refs, keep_prev, cw_ref, xbuf, act_ref):
    tb, halo = DN_BLOCK, DN_PREV_ROWS

    def piece(idx, h):
        sl = slice(h * DN_HEAD_DIM, (h + 1) * DN_HEAD_DIM)
        ref, pref = raw_refs[idx], prev_refs[idx]
        if pref is None:
            xbuf[idx, 0:halo, sl] = jnp.zeros((halo, DN_HEAD_DIM), F32)
        else:
            xbuf[idx, 0:halo, sl] = pref[:, sl].astype(F32) * keep_prev
        x = ref[:, sl].astype(F32)
        xbuf[idx, halo:halo + tb, sl] = x
        w = cw_ref[:, idx * DN_WIDTH + h * DN_HEAD_DIM:idx * DN_WIDTH + (h + 1) * DN_HEAD_DIM]
        y = w[DN_CONV - 1:DN_CONV, :] * x
        for s in range(1, DN_CONV):
            y = y + w[DN_CONV - 1 - s:DN_CONV - s, :] * xbuf[idx, halo - s:halo - s + tb, sl]
        y = _silu(y)
        if idx < 2:
            scale = DN_HEAD_DIM ** -0.5 if idx == 0 else 1.0
            y = y * (lax.rsqrt(jnp.sum(y * y, axis=-1, keepdims=True) + EPS) * scale)
        act_ref[idx, :, sl] = y

    return [functools.partial(piece, idx, h) for idx in range(3) for h in range(DN_HEADS)]


def _dn_intra_kernel(q_ref, k_ref, v_ref, qp_ref, kp_ref, vp_ref, gate_ref, cw_ref, mask_ref,
                     u_ref, w_ref, qd_ref, kd_ref, qk_ref, egl_ref, xbuf, act_cur, *, blocks_per_seq):
    i = pl.program_id(0)
    tb, c = DN_BLOCK, DN_CHUNK
    n_chunks = tb // c

    keep_prev = jnp.where((i % blocks_per_seq) == 0, 0.0, 1.0)
    for piece in _dn_conv_pieces((q_ref, k_ref, v_ref), (qp_ref, kp_ref, vp_ref), keep_prev,
                                 cw_ref, xbuf, act_cur):
        piece()
    pending = []

    un = DN_UNIT
    units = [slice(p * un, (p + 1) * un) for p in range(tb // un)]
    gates = gate_ref[...]
    gcum = jnp.concatenate(
        [jnp.dot(mask_ref[M_INCL], gates[rows], precision=lax.Precision.HIGHEST, preferred_element_type=F32)
         for rows in units], axis=0)
    gcum_t = gcum.T
    gcum2 = gcum * LOG2E
    gcum2_t = gcum_t * LOG2E
    glast = jnp.concatenate(
        [jnp.broadcast_to(gcum[ci * c + c - 1:ci * c + c, :], (c, LANES)) for ci in range(n_chunks)], axis=0)
    e_cum = jnp.exp(gcum)
    e_rem = jnp.exp(glast - gcum)
    for ci in range(n_chunks):
        gl = gcum_t[0:DN_HEADS, ci * c + c - 1:ci * c + c]
        egl_ref[ci * DN_HEADS:(ci + 1) * DN_HEADS, :] = jnp.broadcast_to(jnp.exp(gl), (DN_HEADS, LANES))

    def head_setup(h, part, rhs, ls):
        sl = slice(h * DN_HEAD_DIM, (h + 1) * DN_HEAD_DIM)
        span = slice(part[0].start, part[-1].stop)
        qh, kh, vh = act_cur[0, span, sl], act_cur[1, span, sl], act_cur[2, span, sl]
        beta = jnp.broadcast_to(gates[span, DN_HEADS + h:DN_HEADS + h + 1], kh.shape)
        eg = jnp.broadcast_to(e_cum[span, h:h + 1], kh.shape)
        kb = kh * beta
        qd_ref[span, sl] = (qh * eg).astype(BF16)
        kd_ref[span, sl] = (kh * e_rem[span, h:h + 1]).astype(BF16)
        rhs_h = jnp.concatenate([vh * beta, kb * eg], axis=1).astype(BF16)
        for rows in part:
            loc = slice(rows.start - span.start, rows.stop - span.start)
            rhs.append(rhs_h[loc])
            a = _mm_nt(jnp.concatenate([kb[loc], qh[loc]], axis=0), kh[loc])
            decay = jnp.exp2(gcum2[rows, h:h + 1] - gcum2_t[h:h + 1, rows] + mask_ref[M_NEG_INCL])
            ls.append(a[:un] * (decay * mask_ref[M_STRICT]))
            qk_ref[rows, sl] = (a[un:] * decay).astype(BF16)

    def filler():
        if pending:
            pending.pop(0)()

    heads = range(DN_HEADS)
    for u0 in range(0, len(units), DN_PASS_UNITS):
        part = units[u0:u0 + DN_PASS_UNITS]
        rhs, ls = [], []
        for h in heads:
            head_setup(h, part, rhs, ls)
        tinvs = _inv_unit_lower(ls, mask_ref, filler)
        where = [(h, rows) for h in heads for rows in part]
        for (h, rows), tinv, r in zip(where, tinvs, rhs):
            sl = slice(h * DN_HEAD_DIM, (h + 1) * DN_HEAD_DIM)
            uw = jnp.dot(tinv.astype(BF16), r, preferred_element_type=F32)
            u_ref[rows, sl] = uw[:, :DN_HEAD_DIM]
            w_ref[rows, sl] = uw[:, DN_HEAD_DIM:].astype(BF16)


def _dn_scan_kernel(u_ref, w_ref, qd_ref, kd_ref, qk_ref, z_ref, egl_ref, nw_ref, o_ref, s_scr, *, tb):
    t = pl.program_id(1)
    c = DN_CHUNK

    @pl.when(t == 0)
    def _():
        s_scr[...] = jnp.zeros_like(s_scr)

    nw = nw_ref[...]
    zeros_v = jnp.zeros((c, DN_HEAD_DIM), BF16)
    heads = range(DN_HEADS)
    sls = [slice(h * DN_HEAD_DIM, (h + 1) * DN_HEAD_DIM) for h in heads]
    dot = functools.partial(jnp.dot, preferred_element_type=F32)

    def out_matmuls(rows, ws, pads):
        return [w_s[c:] + dot(qk_ref[rows, sl], v_pad) for sl, w_s, v_pad in zip(sls, ws, pads)]

    def out_finish(rows, outs):
        for sl, o in zip(sls, outs):
            o = o * lax.rsqrt(jnp.mean(o * o, axis=-1, keepdims=True) + EPS) * nw
            o_ref[rows, sl] = (o * _silu(z_ref[rows, sl].astype(F32))).astype(BF16)

    pending = None
    for ci in range(tb // c):
        rows = slice(ci * c, (ci + 1) * c)
        s_old = [s_scr[h] for h in heads]
        ws = [dot(jnp.concatenate([w_ref[rows, sl], qd_ref[rows, sl]], axis=0), s.astype(BF16))
              for sl, s in zip(sls, s_old)]
        prev_outs = out_matmuls(*pending) if pending else None
        vbs = [(u_ref[rows, sl] - w_s[:c]).astype(BF16) for sl, w_s in zip(sls, ws)]
        pads = [jnp.concatenate([vb, zeros_v] if ci % 2 == 0 else [zeros_v, vb], axis=0) for vb in vbs]
        kd_ts = [kd_ref[rows, sl].astype(F32).T.astype(BF16) for sl in sls]
        for h, s, vb, kd_t in zip(heads, s_old, vbs, kd_ts):
            egl = egl_ref[ci * DN_HEADS + h:ci * DN_HEADS + h + 1, :]
            s_scr[h] = s * egl + dot(kd_t, vb)
        if pending:
            out_finish(pending[0], prev_outs)
        pending = (rows, ws, pads)
    out_finish(pending[0], out_matmuls(*pending))


def _deltanet(proj, gates, conv_w, norm_w, layer, batch, seq):
    m = proj.shape[0]
    tb = DN_BLOCK
    nblk = m // tb
    masks = jnp.asarray(_dn_masks())
    prev_per_blk = tb // DN_PREV_ROWS
    cols = (COL_Q // DN_WIDTH, COL_K // DN_WIDTH, COL_V // DN_WIDTH)
    cur = [pl.BlockSpec((tb, DN_WIDTH), lambda i, cb=cb: (i, cb)) for cb in cols]
    prev = [pl.BlockSpec((DN_PREV_ROWS, DN_WIDTH),
                         lambda i, cb=cb: (jnp.maximum(i * prev_per_blk - 1, 0), cb)) for cb in cols]
    tok = pl.BlockSpec((tb, DN_WIDTH), lambda i: (i, 0))
    egl_rows = (tb // DN_CHUNK) * DN_HEADS
    bf_out = jax.ShapeDtypeStruct((m, DN_WIDTH), BF16)
    u, w, qd, kd, qk, egl = pl.pallas_call(
        functools.partial(_dn_intra_kernel, blocks_per_seq=seq // tb),
        grid=(nblk,),
        in_specs=cur + prev + [
            pl.BlockSpec((tb, LANES), lambda i: (i, 0)),
            pl.BlockSpec((None, DN_CONV, 3 * DN_WIDTH), lambda i: (layer, 0, 0)),
            pl.BlockSpec(masks.shape, lambda i: (0, 0, 0))],
        out_specs=[tok, tok, tok, tok, tok, pl.BlockSpec((egl_rows, LANES), lambda i: (i, 0))],
        out_shape=[jax.ShapeDtypeStruct((m, DN_WIDTH), F32), bf_out, bf_out, bf_out, bf_out,
                   jax.ShapeDtypeStruct((m // DN_CHUNK * DN_HEADS, LANES), F32)],
        scratch_shapes=[pltpu.VMEM((3, tb + DN_PREV_ROWS, DN_WIDTH), F32),
                        pltpu.VMEM((3, tb, DN_WIDTH), F32)],
        compiler_params=_params("parallel"),
        name="dn_intra",
    )(*([proj] * 6), gates, conv_w, masks)

    ts = 512
    nt = seq // ts
    blk = lambda cb: pl.BlockSpec((ts, DN_WIDTH), lambda b, t: (b * nt + t, cb))
    return pl.pallas_call(
        functools.partial(_dn_scan_kernel, tb=ts),
        grid=(batch, nt),
        in_specs=[blk(0), blk(0), blk(0), blk(0), blk(0), blk(COL_Z // DN_WIDTH),
                  pl.BlockSpec((ts // DN_CHUNK * DN_HEADS, LANES), lambda b, t: (b * nt + t, 0)),
                  pl.BlockSpec((None, 1, DN_HEAD_DIM), lambda b, t: (layer, 0, 0))],
        out_specs=blk(0),
        out_shape=bf_out,
        scratch_shapes=[pltpu.VMEM((DN_HEADS, DN_HEAD_DIM, DN_HEAD_DIM), F32)],
        compiler_params=_params("arbitrary", "arbitrary"),
        name="dn_scan",
    )(u, w, qd, kd, qk, proj, egl, norm_w)


SWA_TILE = 2 * LANES


def _swa_consts():
    r = np.arange(SWA_TILE)[:, None]
    c = np.arange(SWA_TILE)[None, :]
    ones = (r // SWA_HEAD_DIM) == (c // SWA_HEAD_DIM)
    cl = c % SWA_HEAD_DIM
    perm = ((cl < ROPE_HALF) & (r == c + ROPE_HALF)) | ((cl >= ROPE_HALF) & (cl < ROPE_DIM) & (r == c - ROPE_HALF))
    return np.stack([ones, perm]).astype(np.float32)


def _norm_rope_tiles(tiles, nws, coss, sins, const_ref):
    dot = functools.partial(jnp.dot, preferred_element_type=F32)
    ws = [t.shape[1] for t in tiles]
    ms = [dot((t * t).astype(BF16), const_ref[0, :w, :w]) for t, w in zip(tiles, ws)]
    ys = [t * lax.rsqrt(m * (1.0 / SWA_HEAD_DIM) + EPS) * nw for t, m, nw in zip(tiles, ms, nws)]
    partners = [dot(y.astype(BF16), const_ref[1, :w, :w]) for y, w in zip(ys, ws)]
    return [y * cs[:, :w] + p * sn[:, :w] for y, p, w, cs, sn in zip(ys, partners, ws, coss, sins)]


def _swa_kernel(sink_ref, q_ref, k_ref, v_ref, cos_ref, sin_ref, qn_ref, kn_ref, const_ref, o_ref,
                kprev, vprev):
    n = pl.program_id(1)
    blk = SWA_BLOCK

    @pl.when(n == 0)
    def _():
        kprev[...] = jnp.zeros_like(kprev)
        vprev[...] = jnp.zeros_like(vprev)

    n_qt = SWA_WIDTH // SWA_TILE
    blocks = [slice(sb * blk, (sb + 1) * blk) for sb in range(SWA_STEP // blk)]
    tiles, nws, coss, sins = [], [], [], []
    for rows in blocks:
        cos_t = jnp.tile(cos_ref[rows, :], (1, SWA_TILE // LANES))
        sin_t = jnp.tile(sin_ref[rows, :], (1, SWA_TILE // LANES))
        tiles += [q_ref[rows, t * SWA_TILE:(t + 1) * SWA_TILE].astype(F32) for t in range(n_qt)]
        tiles.append(k_ref[rows, :].astype(F32))
        nws += [qn_ref[:, t * SWA_TILE:(t + 1) * SWA_TILE] for t in range(n_qt)] + [kn_ref[...]]
        coss += [cos_t] * (n_qt + 1)
        sins += [sin_t] * (n_qt + 1)
    roped = _norm_rope_tiles(tiles, nws, coss, sins, const_ref)

    for sb, rows in enumerate(blocks):
        mine = roped[sb * (n_qt + 1):(sb + 1) * (n_qt + 1)]
        qb = jnp.concatenate([(t * (SWA_HEAD_DIM ** -0.5 * LOG2E)).astype(BF16) for t in mine[:n_qt]], axis=1)
        prev_penalty = jnp.where(n > 0, 0.0, NEG_BIG) if sb == 0 else 0.0
        _swa_block(rows, prev_penalty, qb, mine[n_qt], sink_ref, v_ref, o_ref, kprev, vprev)


def _swa_block(rows, prev_penalty, qb, k_roped, sink_ref, v_ref, o_ref, kprev, vprev):
    blk = SWA_BLOCK

    low = lax.broadcasted_iota(jnp.int32, (blk, LANES), 1) < SWA_HEAD_DIM

    def split_heads(cur, prev_ref):
        swap = pltpu.roll(cur, SWA_HEAD_DIM, axis=1)
        parts = [jnp.where(low, cur, 0.0), jnp.where(low, swap, 0.0),
                 jnp.where(low, 0.0, swap), jnp.where(low, 0.0, cur)]
        bands = []
        for idx, part in enumerate(parts):
            part = part.astype(BF16)
            bands.append(jnp.concatenate([prev_ref[idx], part], axis=0))
            prev_ref[idx] = part
        return bands

    k_bands = split_heads(k_roped, kprev)
    v_bands = split_heads(v_ref[rows, :].astype(F32), vprev)

    qi = lax.broadcasted_iota(jnp.int32, (blk, 2 * blk), 0)
    kj = lax.broadcasted_iota(jnp.int32, (blk, 2 * blk), 1)
    rel = qi + blk - kj
    bias = jnp.where(rel >= 0, jnp.where(rel < blk, 0.0, NEG_BIG), NEG_BIG)
    bias = bias + jnp.where(kj < blk, prev_penalty, 0.0)

    group = SWA_Q_HEADS // SWA_KV_HEADS
    for h0 in range(0, SWA_Q_HEADS, SWA_HEAD_BATCH):
        heads = range(h0, h0 + SWA_HEAD_BATCH)
        kops = [k_bands[2 * (h % 2) + h // group] for h in heads]
        vops = [v_bands[2 * (h % 2) + h // group] for h in heads]
        sinks = [sink_ref[h] * LOG2E for h in heads]
        ss = [_mm_nt(qb[:, (h // 2) * LANES:(h // 2 + 1) * LANES], kop) + bias
              for h, kop in zip(heads, kops)]
        mxs = [jnp.maximum(jnp.max(s, axis=-1, keepdims=True), sink) for s, sink in zip(ss, sinks)]
        ps = [jnp.exp2(s - mx) for s, mx in zip(ss, mxs)]
        denoms = [jnp.sum(p, axis=-1, keepdims=True) + jnp.exp2(sink - mx)
                  for p, sink, mx in zip(ps, sinks, mxs)]
        outs = [_mm(p, vop) * (1.0 / d) for p, vop, d in zip(ps, vops, denoms)]
        for pair in range(h0 // 2, (h0 + SWA_HEAD_BATCH) // 2):
            o_ref[rows, pair * LANES:(pair + 1) * LANES] = (
                outs[2 * pair - h0] + outs[2 * pair + 1 - h0]).astype(BF16)


def _swa(proj, cos_t, sin_t, sinks, qn, kn, layer, batch, seq):
    m = proj.shape[0]
    blk, step = SWA_BLOCK, SWA_STEP
    nb = seq // step
    row = lambda b, n: b * nb + n
    consts = jnp.asarray(_swa_consts(), BF16)
    return pl.pallas_call(
        _swa_kernel,
        grid=(batch, nb),
        in_specs=[pl.BlockSpec(memory_space=pltpu.SMEM),
                  pl.BlockSpec((step, SWA_WIDTH), lambda b, n: (row(b, n), COL_SWQ // SWA_WIDTH)),
                  pl.BlockSpec((step, SWA_KV_WIDTH), lambda b, n: (row(b, n), COL_SWK // SWA_KV_WIDTH)),
                  pl.BlockSpec((step, SWA_KV_WIDTH), lambda b, n: (row(b, n), COL_SWV // SWA_KV_WIDTH)),
                  pl.BlockSpec((step, LANES), lambda b, n: (row(b, n), 0)),
                  pl.BlockSpec((step, LANES), lambda b, n: (row(b, n), 0)),
                  pl.BlockSpec((None, 1, SWA_WIDTH), lambda b, n: (layer, 0, 0)),
                  pl.BlockSpec((None, 1, SWA_KV_WIDTH), lambda b, n: (layer, 0, 0)),
                  pl.BlockSpec(consts.shape, lambda b, n: (0, 0, 0))],
        out_specs=pl.BlockSpec((step, SWA_WIDTH), lambda b, n: (row(b, n), 0)),
        out_shape=jax.ShapeDtypeStruct((m, SWA_WIDTH), BF16),
        scratch_shapes=[pltpu.VMEM((2 * SWA_KV_HEADS, blk, SWA_KV_WIDTH), BF16),
                        pltpu.VMEM((2 * SWA_KV_HEADS, blk, SWA_KV_WIDTH), BF16)],
        compiler_params=_params("arbitrary", "arbitrary"),
        name="swa",
    )(sinks, proj, proj, proj, cos_t, sin_t, qn, kn, consts)


def _merge_kernel(x_ref, odn_ref, osw_ref, ga_ref, gb_ref, gt_ref, wdn_ref, wsw_ref, wo_ref, o_ref):
    ya = jnp.dot(odn_ref[...], wdn_ref[...], preferred_element_type=F32)
    yb = jnp.dot(osw_ref[...], wsw_ref[...], preferred_element_type=F32)
    merged = _sigmoid(ga_ref[...].astype(F32)) * ya + _sigmoid(gb_ref[...].astype(F32)) * yb
    out = jnp.dot(merged.astype(BF16), wo_ref[...], preferred_element_type=F32)
    o_ref[...] = x_ref[...] + gt_ref[0] * out


def _merge(x, o_dn, o_sw, proj, mod, w_dn, w_sw, w_o, layer, seq):
    m = x.shape[0]
    tm = 1024
    tiles_per_seq = seq // tm
    tok = lambda cb: pl.BlockSpec((tm, D_MODEL), lambda i: (i, cb))
    wfull = pl.BlockSpec((None, D_MODEL, D_MODEL), lambda i: (layer, 0, 0))
    return pl.pallas_call(
        _merge_kernel,
        grid=(m // tm,),
        in_specs=[tok(0), tok(0), tok(0), tok(COL_GA // D_MODEL), tok(COL_GB // D_MODEL),
                  _mod_spec(layer, 2, tiles_per_seq), wfull, wfull, wfull],
        out_specs=tok(0),
        out_shape=jax.ShapeDtypeStruct((m, D_MODEL), F32),
        compiler_params=_params("arbitrary"),
        name="merge_out",
    )(x, o_dn, o_sw, proj, proj, mod, w_dn, w_sw, w_o)


def _ffn_kernel(x_ref, nw_ref, sc_ref, sh_ref, gt_ref, wa_ref, wl_ref, cw_ref, cb_ref, wd_ref,
                o_ref, h_scr, acc_scr, abuf, halo_scr, *, tm, tiles_per_seq, n_ff_tiles):
    i = pl.program_id(0)
    j = pl.program_id(1)
    halo = SUBLANES

    @pl.when(j == 0)
    def _():
        h_scr[...] = _norm_mod(x_ref[...], nw_ref[...], sc_ref[0], sh_ref[0]).astype(BF16)

    h = h_scr[...]
    a = jnp.dot(h, wa_ref[...], preferred_element_type=F32)
    lin = jnp.dot(h, wl_ref[...], preferred_element_type=F32)

    first = (i % tiles_per_seq) == 0
    prev = halo_scr[j]
    abuf[0:halo, :] = jnp.where(first, jnp.zeros_like(prev), prev)
    abuf[halo:halo + tm, :] = a
    halo_scr[j] = a[tm - halo:tm, :]
    w = cw_ref[...]
    y = w[FFN_CONV - 1:FFN_CONV, :] * a + cb_ref[...]
    for s in range(1, FFN_CONV):
        y = y + w[FFN_CONV - 1 - s:FFN_CONV - s, :] * abuf[halo - s:halo - s + tm, :]
    act = (_silu(y) * lin).astype(BF16)
    part = jnp.dot(act, wd_ref[...], preferred_element_type=F32)

    @pl.when(j == 0)
    def _():
        acc_scr[...] = part

    @pl.when(j > 0)
    def _():
        acc_scr[...] += part

    @pl.when(j == n_ff_tiles - 1)
    def _():
        o_ref[...] = x_ref[...] + gt_ref[0] * acc_scr[...]


def _ffn(x, nw, mod, w_up, conv_w, conv_b, w_down, layer, seq):
    m = x.shape[0]
    tm = 512
    n_ff = 2
    fc = D_FF // n_ff
    tiles_per_seq = seq // tm
    kern = functools.partial(_ffn_kernel, tm=tm, tiles_per_seq=tiles_per_seq, n_ff_tiles=n_ff)
    return pl.pallas_call(
        kern,
        grid=(m // tm, n_ff),
        in_specs=[pl.BlockSpec((tm, D_MODEL), lambda i, j: (i, 0)),
                  pl.BlockSpec((None, 1, D_MODEL), lambda i, j: (layer, 0, 0)),
                  _mod_spec(layer, 4, tiles_per_seq), _mod_spec(layer, 3, tiles_per_seq),
                  _mod_spec(layer, 5, tiles_per_seq),
                  pl.BlockSpec((None, D_MODEL, fc), lambda i, j: (layer, 0, j)),
                  pl.BlockSpec((None, D_MODEL, fc), lambda i, j: (layer, 0, n_ff + j)),
                  pl.BlockSpec((None, FFN_CONV, fc), lambda i, j: (layer, 0, j)),
                  pl.BlockSpec((None, 1, fc), lambda i, j: (layer, 0, j)),
                  pl.BlockSpec((None, fc, D_MODEL), lambda i, j: (layer, j, 0))],
        out_specs=pl.BlockSpec((tm, D_MODEL), lambda i, j: (i, 0)),
        out_shape=jax.ShapeDtypeStruct((m, D_MODEL), F32),
        scratch_shapes=[pltpu.VMEM((tm, D_MODEL), BF16),
                        pltpu.VMEM((tm, D_MODEL), F32),
                        pltpu.VMEM((tm + SUBLANES, fc), F32),
                        pltpu.VMEM((n_ff, SUBLANES, fc), F32)],
        compiler_params=_params("arbitrary", "arbitrary"),
        name="ffn",
    )(x, nw, mod, mod, mod, w_up, w_up, conv_w, conv_b, w_down)


def _pack_moves():
    o_a = 4 * DN_WIDTH
    o_swq = o_a + 2 * DN_HEADS
    o_swk = o_swq + SWA_WIDTH
    o_swv = o_swk + SWA_KV_WIDTH
    o_ga = o_swv + SWA_KV_WIDTH
    o_gb = o_ga + D_MODEL
    return ((0, COL_Q, o_a), (o_swq, COL_SWQ, SWA_WIDTH), (o_ga, COL_GA, D_MODEL), (o_gb, COL_GB, D_MODEL),
            (o_swk, COL_SWK, SWA_KV_WIDTH), (o_swv, COL_SWV, SWA_KV_WIDTH), (o_a, COL_AB, 2 * DN_HEADS))


def _pack_kernel(w_ref, o_ref):
    for src, dst, width in _pack_moves():
        o_ref[dst:dst + width, :] = w_ref[src:src + width, :].astype(BF16)
    tail = COL_AB + 2 * DN_HEADS
    o_ref[tail:, :] = jnp.zeros((IN_PACKED - tail, o_ref.shape[1]), BF16)


def _pack_w_in(w_in):
    depth, d_in, n_in = w_in.shape
    w_t = jnp.swapaxes(w_in, 1, 2)
    tc = 256
    return pl.pallas_call(
        _pack_kernel,
        grid=(depth, d_in // tc),
        in_specs=[pl.BlockSpec((None, n_in, tc), lambda l, i: (l, 0, i))],
        out_specs=pl.BlockSpec((None, IN_PACKED, tc), lambda l, i: (l, 0, i)),
        out_shape=jax.ShapeDtypeStruct((depth, IN_PACKED, d_in), BF16),
        compiler_params=_params("parallel", "parallel"),
        name="pack_w_in",
    )(w_t)


def _lane_row(v):
    depth, n = v.shape
    return jnp.zeros((depth, 1, LANES), F32).at[:, 0, :n].set(v.astype(F32))


def kernel(x, c, positions, w_ada, b_ada, norm_mix, w_in, dn_conv, dn_a_log, dn_dt_bias, dn_norm,
           w_dn_out, swa_q_norm, swa_k_norm, swa_sinks, w_swa_out, w_o, norm_ffn, w_up, ffn_conv,
           ffn_conv_b, w_down):
    batch, seq, _ = x.shape
    depth = w_ada.shape[0]
    m = batch * seq

    mod_all = _ada_mod(c, w_ada, b_ada)
    cos_t, sin_t = _rope_tables(positions)

    w_in_p = _pack_w_in(w_in)
    w_dn_b, w_sw_b, w_o_b = w_dn_out.astype(BF16), w_swa_out.astype(BF16), w_o.astype(BF16)
    w_up_b, w_down_b = w_up.astype(BF16), w_down.astype(BF16)
    alog = _lane_row(dn_a_log)
    dtb = _lane_row(dn_dt_bias)
    qn = jnp.tile(swa_q_norm, (1, SWA_Q_HEADS)).reshape(depth, 1, SWA_WIDTH)
    kn = jnp.tile(swa_k_norm, (1, SWA_KV_HEADS)).reshape(depth, 1, SWA_KV_WIDTH)

    mod = mod_all.reshape(depth * SUBLANES * 6, 1, D_MODEL)
    norm_mix3 = norm_mix.reshape(depth, 1, D_MODEL)
    norm_ffn3 = norm_ffn.reshape(depth, 1, D_MODEL)
    dn_norm3 = dn_norm.reshape(depth, 1, DN_HEAD_DIM)
    conv_b3 = ffn_conv_b.reshape(depth, 1, D_FF)

    xf = x.reshape(m, D_MODEL)
    for l in range(depth):
        proj, gates = _inproj(xf, norm_mix3, mod, w_in_p, alog, dtb, l, seq)
        o_dn = _deltanet(proj, gates, dn_conv, dn_norm3, l, batch, seq)
        o_sw = _swa(proj, cos_t, sin_t, swa_sinks[l], qn, kn, l, batch, seq)
        xf = _merge(xf, o_dn, o_sw, proj, mod, w_dn_b, w_sw_b, w_o_b, l, seq)
        xf = _ffn(xf, norm_ffn3, mod, w_up_b, ffn_conv, conv_b3, w_down_b, l, seq)
    return xf.reshape(batch, seq, D_MODEL)
```

```python
import functools

import numpy as np
import jax
import jax.numpy as jnp
from jax import lax
from jax.experimental import pallas as pl
from jax.experimental.pallas import tpu as pltpu

F32 = jnp.float32
BF16 = jnp.bfloat16

D_MODEL = 1024
DN_HEADS = 8
DN_HEAD_DIM = 128
DN_WIDTH = DN_HEADS * DN_HEAD_DIM
DN_CONV = 4
DN_CHUNK = 64
SWA_Q_HEADS = 16
SWA_KV_HEADS = 2
SWA_HEAD_DIM = 64
SWA_WIDTH = SWA_Q_HEADS * SWA_HEAD_DIM
SWA_KV_WIDTH = SWA_KV_HEADS * SWA_HEAD_DIM
SWA_BLOCK = 128
SWA_STEP = 512
SWA_HEAD_BATCH = 16
ROPE_THETA = 500000.0
ROPE_DIM = SWA_HEAD_DIM // 4
ROPE_HALF = ROPE_DIM // 2
D_FF = 2816
FFN_CONV = 3
EPS = 1e-6

LANES = 128
SUBLANES = 8
VMEM_LIMIT = 56 * 1024 * 1024

COL_Q, COL_K, COL_V, COL_Z = 0, 1024, 2048, 3072
COL_SWQ, COL_GA, COL_GB = 4096, 5120, 6144
COL_SWK, COL_SWV, COL_AB = 7168, 7296, 7424
IN_PACKED = 7680
NEG_BIG = -1e30


def _sigmoid(x):
    return 1.0 / (1.0 + jnp.exp(-x))


def _silu(x):
    return x * _sigmoid(x)


def _mm(a, b):
    return jnp.dot(a.astype(BF16), b.astype(BF16), preferred_element_type=F32)


def _mm_nt(a, b):
    return lax.dot_general(a.astype(BF16), b.astype(BF16), (((1,), (1,)), ((), ())),
                           preferred_element_type=F32)


def _params(*sem):
    return pltpu.CompilerParams(dimension_semantics=sem, vmem_limit_bytes=VMEM_LIMIT)


def _ada_kernel(c_ref, w_ref, b_ref, o_ref):
    ca = _silu(c_ref[...])
    o_ref[0] = jnp.dot(ca, w_ref[0], precision=lax.Precision.HIGHEST,
                       preferred_element_type=F32) + b_ref[0]


def _ada_mod(c, w_ada, b_ada):
    depth = w_ada.shape[0]
    batch = c.shape[0]
    n_out = w_ada.shape[2]
    tn = 1536
    c_pad = jnp.zeros((SUBLANES, D_MODEL), F32).at[:batch].set(c)
    return pl.pallas_call(
        _ada_kernel,
        grid=(depth, n_out // tn),
        in_specs=[pl.BlockSpec((SUBLANES, D_MODEL), lambda l, j: (0, 0)),
                  pl.BlockSpec((1, D_MODEL, tn), lambda l, j: (l, 0, j)),
                  pl.BlockSpec((1, 1, tn), lambda l, j: (l, 0, j))],
        out_specs=pl.BlockSpec((1, SUBLANES, tn), lambda l, j: (l, 0, j)),
        out_shape=jax.ShapeDtypeStruct((depth, SUBLANES, n_out), F32),
        compiler_params=_params("arbitrary", "arbitrary"),
        name="ada_mod",
    )(c_pad, w_ada, b_ada.reshape(depth, 1, n_out))


def _rope_kernel(pos_ref, inv_ref, sgn_ref, cos_ref, sin_ref):
    ang = pos_ref[...] * inv_ref[...]
    on = sgn_ref[...] != 0.0
    cos_ref[...] = jnp.where(on, jnp.cos(ang), 1.0)
    sin_ref[...] = jnp.sin(ang) * sgn_ref[...]


def _rope_tables(positions):
    m = positions.size
    tm = min(2048, m)
    lane = np.arange(LANES) % SWA_HEAD_DIM
    inv = np.where(lane < ROPE_DIM,
                   np.power(ROPE_THETA, -(lane % ROPE_HALF).astype(np.float64) / ROPE_HALF), 0.0)
    sgn = np.where(lane < ROPE_HALF, -1.0, np.where(lane < ROPE_DIM, 1.0, 0.0))
    pos = positions.astype(F32).reshape(m, 1)
    return pl.pallas_call(
        _rope_kernel,
        grid=(m // tm,),
        in_specs=[pl.BlockSpec((tm, 1), lambda i: (i, 0)),
                  pl.BlockSpec((1, LANES), lambda i: (0, 0)),
                  pl.BlockSpec((1, LANES), lambda i: (0, 0))],
        out_specs=[pl.BlockSpec((tm, LANES), lambda i: (i, 0)),
                   pl.BlockSpec((tm, LANES), lambda i: (i, 0))],
        out_shape=[jax.ShapeDtypeStruct((m, LANES), F32)] * 2,
        compiler_params=_params("arbitrary"),
        name="rope_tables",
    )(pos, jnp.asarray(inv, F32).reshape(1, LANES), jnp.asarray(sgn, F32).reshape(1, LANES))


def _norm_mod(x, nw, sc, sh):
    ms = jnp.mean(x * x, axis=-1, keepdims=True)
    return (x * lax.rsqrt(ms + EPS) * nw) * (1.0 + sc) + sh


def _inproj_kernel(x_ref, nw_ref, sc_ref, sh_ref, w_ref, alog_ref, dtb_ref,
                   proj_ref, gate_ref, h_scr, *, n_col_tiles, ab_off):
    j = pl.program_id(1)

    @pl.when(j == 0)
    def _():
        h_scr[...] = _norm_mod(x_ref[...], nw_ref[...], sc_ref[0], sh_ref[0]).astype(BF16)

    acc = _mm_nt(h_scr[...], w_ref[...])
    proj_ref[...] = acc.astype(BF16)

    @pl.when(j == n_col_tiles - 1)
    def _():
        ab = acc[:, ab_off:ab_off + LANES]
        z = ab + dtb_ref[...]
        softplus = jnp.maximum(z, 0.0) + jnp.log(1.0 + jnp.exp(-jnp.abs(z)))
        g = -jnp.exp(alog_ref[...]) * softplus
        lane = lax.broadcasted_iota(jnp.int32, ab.shape, 1)
        gate_ref[...] = jnp.where(lane < DN_HEADS, g, _sigmoid(ab))


def _mod_spec(layer, k, tiles_per_seq):
    return pl.BlockSpec((1, 1, D_MODEL),
                        lambda i, *_: ((layer * SUBLANES + i // tiles_per_seq) * 6 + k, 0, 0))


def _inproj(x, nw, mod, w, alog, dtb, layer, seq):
    m = x.shape[0]
    tm, tn = 1024, 2560
    nj = IN_PACKED // tn
    tiles_per_seq = seq // tm
    kern = functools.partial(_inproj_kernel, n_col_tiles=nj, ab_off=COL_AB - (nj - 1) * tn)
    return pl.pallas_call(
        kern,
        grid=(m // tm, nj),
        in_specs=[pl.BlockSpec((tm, D_MODEL), lambda i, j: (i, 0)),
                  pl.BlockSpec((None, 1, D_MODEL), lambda i, j: (layer, 0, 0)),
                  _mod_spec(layer, 1, tiles_per_seq), _mod_spec(layer, 0, tiles_per_seq),
                  pl.BlockSpec((None, tn, D_MODEL), lambda i, j: (layer, j, 0)),
                  pl.BlockSpec((None, 1, LANES), lambda i, j: (layer, 0, 0)),
                  pl.BlockSpec((None, 1, LANES), lambda i, j: (layer, 0, 0))],
        out_specs=[pl.BlockSpec((tm, tn), lambda i, j: (i, j)),
                   pl.BlockSpec((tm, LANES), lambda i, j: (i, 0))],
        out_shape=[jax.ShapeDtypeStruct((m, IN_PACKED), BF16),
                   jax.ShapeDtypeStruct((m, LANES), F32)],
        scratch_shapes=[pltpu.VMEM((tm, D_MODEL), BF16)],
        compiler_params=_params("arbitrary", "arbitrary"),
        name="inproj",
    )(x, nw, mod, mod, w, alog, dtb)


DN_BLOCK = 512
DN_PASS_UNITS = 2
DN_UNIT = 128
INV_BLOCK = 16
DN_PREV_ROWS = 16
M_INCL, M_STRICT, M_DIAG, M_EYE, M_NEG_INCL, M_OFF0 = 0, 1, 2, 3, 4, 5
LOG2E = float(np.log2(np.e))


def _dn_masks():
    r = np.arange(DN_UNIT)[:, None]
    c = np.arange(DN_UNIT)[None, :]
    same = lambda b: (r // b) == (c // b)
    chunk = same(DN_CHUNK)
    incl = chunk & (r >= c)
    masks = [incl, chunk & (r > c), same(INV_BLOCK), r == c, np.where(incl, 0.0, NEG_BIG)]
    b = INV_BLOCK
    while b < DN_CHUNK:
        masks.append(same(2 * b) & ~same(b))
        b *= 2
    return np.stack(masks).astype(np.float32)


def _inv_unit_lower(l_strict, mask_ref, filler):
    dot = functools.partial(jnp.dot, preferred_element_type=F32)
    ds = [l * mask_ref[M_DIAG] for l in l_strict]
    ts = [mask_ref[M_EYE] - d for d in ds]
    dbs = [d.astype(BF16) for d in ds]
    mpows = [dot(db, db) for db in dbs]
    filler()
    n_fac = int(np.log2(INV_BLOCK)) - 1
    for i in range(n_fac):
        mbs = [m.astype(BF16) for m in mpows]
        ts = [t + dot(t.astype(BF16), mb) for t, mb in zip(ts, mbs)]
        if i < n_fac - 1:
            mpows = [dot(mb, mb) for mb in mbs]
        filler()
    n_levels = int(np.log2(DN_CHUNK // INV_BLOCK))
    for lvl in range(n_levels):
        tbs = [t.astype(BF16) for t in ts]
        inner = [dot((l * mask_ref[M_OFF0 + lvl]).astype(BF16), tb).astype(BF16)
                 for l, tb in zip(l_strict, tbs)]
        filler()
        ts = [t - dot(tb, inn) for t, tb, inn in zip(ts, tbs, inner)]
        filler()
    return ts


def _dn_conv_pieces(raw_refs, prev_refs, keep_prev, cw_ref, xbuf, act_ref):
    tb, halo = DN_BLOCK, DN_PREV_ROWS

    def piece(idx, h):
        sl = slice(h * DN_HEAD_DIM, (h + 1) * DN_HEAD_DIM)
        ref, pref = raw_refs[idx], prev_refs[idx]
        if pref is None:
            xbuf[idx, 0:halo, sl] = jnp.zeros((halo, DN_HEAD_DIM), F32)
        else:
            xbuf[idx, 0:halo, sl] = pref[:, sl].astype(F32) * keep_prev
        x = ref[:, sl].astype(F32)
        xbuf[idx, halo:halo + tb, sl] = x
        w = cw_ref[:, idx * DN_WIDTH + h * DN_HEAD_DIM:idx * DN_WIDTH + (h + 1) * DN_HEAD_DIM]
        y = w[DN_CONV - 1:DN_CONV, :] * x
        for s in range(1, DN_CONV):
            y = y + w[DN_CONV - 1 - s:DN_CONV - s, :] * xbuf[idx, halo - s:halo - s + tb, sl]
        y = _silu(y)
        if idx < 2:
            scale = DN_HEAD_DIM ** -0.5 if idx == 0 else 1.0
            y = y * (lax.rsqrt(jnp.sum(y * y, axis=-1, keepdims=True) + EPS) * scale)
        act_ref[idx, :, sl] = y

    return [functools.partial(piece, idx, h) for idx in range(3) for h in range(DN_HEADS)]


def _dn_intra_kernel(q_ref, k_ref, v_ref, qp_ref, kp_ref, vp_ref, gate_ref, cw_ref, mask_ref,
                     u_ref, w_ref, qd_ref, kd_ref, qk_ref, egl_ref, xbuf, act_cur, *, blocks_per_seq):
    i = pl.program_id(0)
    tb, c = DN_BLOCK, DN_CHUNK
    n_chunks = tb // c

    keep_prev = jnp.where((i % blocks_per_seq) == 0, 0.0, 1.0)
    for piece in _dn_conv_pieces((q_ref, k_ref, v_ref), (qp_ref, kp_ref, vp_ref), keep_prev,
                                 cw_ref, xbuf, act_cur):
        piece()
    pending = []

    un = DN_UNIT
    units = [slice(p * un, (p + 1) * un) for p in range(tb // un)]
    gates = gate_ref[...]
    gcum = jnp.concatenate(
        [jnp.dot(mask_ref[M_INCL], gates[rows], precision=lax.Precision.HIGHEST, preferred_element_type=F32)
         for rows in units], axis=0)
    gcum_t = gcum.T
    gcum2 = gcum * LOG2E
    gcum2_t = gcum_t * LOG2E
    glast = jnp.concatenate(
        [jnp.broadcast_to(gcum[ci * c + c - 1:ci * c + c, :], (c, LANES)) for ci in range(n_chunks)], axis=0)
    e_cum = jnp.exp(gcum)
    e_rem = jnp.exp(glast - gcum)
    for ci in range(n_chunks):
        gl = gcum_t[0:DN_HEADS, ci * c + c - 1:ci * c + c]
        egl_ref[ci * DN_HEADS:(ci + 1) * DN_HEADS, :] = jnp.broadcast_to(jnp.exp(gl), (DN_HEADS, LANES))

    def head_setup(h, part, rhs, ls):
        sl = slice(h * DN_HEAD_DIM, (h + 1) * DN_HEAD_DIM)
        span = slice(part[0].start, part[-1].stop)
        qh, kh, vh = act_cur[0, span, sl], act_cur[1, span, sl], act_cur[2, span, sl]
        beta = jnp.broadcast_to(gates[span, DN_HEADS + h:DN_HEADS + h + 1], kh.shape)
        eg = jnp.broadcast_to(e_cum[span, h:h + 1], kh.shape)
        kb = kh * beta
        qd_ref[span, sl] = (qh * eg).astype(BF16)
        kd_ref[span, sl] = (kh * e_rem[span, h:h + 1]).astype(BF16)
        rhs_h = jnp.concatenate([vh * beta, kb * eg], axis=1).astype(BF16)
        for rows in part:
            loc = slice(rows.start - span.start, rows.stop - span.start)
            rhs.append(rhs_h[loc])
            a = _mm_nt(jnp.concatenate([kb[loc], qh[loc]], axis=0), kh[loc])
            decay = jnp.exp2(gcum2[rows, h:h + 1] - gcum2_t[h:h + 1, rows] + mask_ref[M_NEG_INCL])
            ls.append(a[:un] * (decay * mask_ref[M_STRICT]))
            qk_ref[rows, sl] = (a[un:] * decay).astype(BF16)

    def filler():
        if pending:
            pending.pop(0)()

    heads = range(DN_HEADS)
    for u0 in range(0, len(units), DN_PASS_UNITS):
        part = units[u0:u0 + DN_PASS_UNITS]
        rhs, ls = [], []
        for h in heads:
            head_setup(h, part, rhs, ls)
        tinvs = _inv_unit_lower(ls, mask_ref, filler)
        where = [(h, rows) for h in heads for rows in part]
        for (h, rows), tinv, r in zip(where, tinvs, rhs):
            sl = slice(h * DN_HEAD_DIM, (h + 1) * DN_HEAD_DIM)
            uw = jnp.dot(tinv.astype(BF16), r, preferred_element_type=F32)
            u_ref[rows, sl] = uw[:, :DN_HEAD_DIM]
            w_ref[rows, sl] = uw[:, DN_HEAD_DIM:].astype(BF16)


def _dn_scan_kernel(u_ref, w_ref, qd_ref, kd_ref, qk_ref, z_ref, egl_ref, nw_ref, o_ref, s_scr, *, tb):
    t = pl.program_id(1)
    c = DN_CHUNK

    @pl.when(t == 0)
    def _():
        s_scr[...] = jnp.zeros_like(s_scr)

    nw = nw_ref[...]
    zeros_v = jnp.zeros((c, DN_HEAD_DIM), BF16)
    heads = range(DN_HEADS)
    sls = [slice(h * DN_HEAD_DIM, (h + 1) * DN_HEAD_DIM) for h in heads]
    dot = functools.partial(jnp.dot, preferred_element_type=F32)

    def out_matmuls(rows, ws, pads):
        return [w_s[c:] + dot(qk_ref[rows, sl], v_pad) for sl, w_s, v_pad in zip(sls, ws, pads)]

    def out_finish(rows, outs):
        for sl, o in zip(sls, outs):
            o = o * lax.rsqrt(jnp.mean(o * o, axis=-1, keepdims=True) + EPS) * nw
            o_ref[rows, sl] = (o * _silu(z_ref[rows, sl].astype(F32))).astype(BF16)

    pending = None
    for ci in range(tb // c):
        rows = slice(ci * c, (ci + 1) * c)
        s_old = [s_scr[h] for h in heads]
        ws = [dot(jnp.concatenate([w_ref[rows, sl], qd_ref[rows, sl]], axis=0), s.astype(BF16))
              for sl, s in zip(sls, s_old)]
        prev_outs = out_matmuls(*pending) if pending else None
        vbs = [(u_ref[rows, sl] - w_s[:c]).astype(BF16) for sl, w_s in zip(sls, ws)]
        pads = [jnp.concatenate([vb, zeros_v] if ci % 2 == 0 else [zeros_v, vb], axis=0) for vb in vbs]
        kd_ts = [kd_ref[rows, sl].astype(F32).T.astype(BF16) for sl in sls]
        for h, s, vb, kd_t in zip(heads, s_old, vbs, kd_ts):
            egl = egl_ref[ci * DN_HEADS + h:ci * DN_HEADS + h + 1, :]
            s_scr[h] = s * egl + dot(kd_t, vb)
        if pending:
            out_finish(pending[0], prev_outs)
        pending = (rows, ws, pads)
    out_finish(pending[0], out_matmuls(*pending))


def _deltanet(proj, gates, conv_w, norm_w, layer, batch, seq):
    m = proj.shape[0]
    tb = DN_BLOCK
    nblk = m // tb
    masks = jnp.asarray(_dn_masks())
    prev_per_blk = tb // DN_PREV_ROWS
    cols = (COL_Q // DN_WIDTH, COL_K // DN_WIDTH, COL_V // DN_WIDTH)
    cur = [pl.BlockSpec((tb, DN_WIDTH), lambda i, cb=cb: (i, cb)) for cb in cols]
    prev = [pl.BlockSpec((DN_PREV_ROWS, DN_WIDTH),
                         lambda i, cb=cb: (jnp.maximum(i * prev_per_blk - 1, 0), cb)) for cb in cols]
    tok = pl.BlockSpec((tb, DN_WIDTH), lambda i: (i, 0))
    egl_rows = (tb // DN_CHUNK) * DN_HEADS
    bf_out = jax.ShapeDtypeStruct((m, DN_WIDTH), BF16)
    u, w, qd, kd, qk, egl = pl.pallas_call(
        functools.partial(_dn_intra_kernel, blocks_per_seq=seq // tb),
        grid=(nblk,),
        in_specs=cur + prev + [
            pl.BlockSpec((tb, LANES), lambda i: (i, 0)),
            pl.BlockSpec((None, DN_CONV, 3 * DN_WIDTH), lambda i: (layer, 0, 0)),
            pl.BlockSpec(masks.shape, lambda i: (0, 0, 0))],
        out_specs=[tok, tok, tok, tok, tok, pl.BlockSpec((egl_rows, LANES), lambda i: (i, 0))],
        out_shape=[jax.ShapeDtypeStruct((m, DN_WIDTH), F32), bf_out, bf_out, bf_out, bf_out,
                   jax.ShapeDtypeStruct((m // DN_CHUNK * DN_HEADS, LANES), F32)],
        scratch_shapes=[pltpu.VMEM((3, tb + DN_PREV_ROWS, DN_WIDTH), F32),
                        pltpu.VMEM((3, tb, DN_WIDTH), F32)],
        compiler_params=_params("parallel"),
        name="dn_intra",
    )(*([proj] * 6), gates, conv_w, masks)

    ts = 1024
    nt = seq // ts
    blk = lambda cb: pl.BlockSpec((ts, DN_WIDTH), lambda b, t: (b * nt + t, cb))
    return pl.pallas_call(
        functools.partial(_dn_scan_kernel, tb=ts),
        grid=(batch, nt),
        in_specs=[blk(0), blk(0), blk(0), blk(0), blk(0), blk(COL_Z // DN_WIDTH),
                  pl.BlockSpec((ts // DN_CHUNK * DN_HEADS, LANES), lambda b, t: (b * nt + t, 0)),
                  pl.BlockSpec((None, 1, DN_HEAD_DIM), lambda b, t: (layer, 0, 0))],
        out_specs=blk(0),
        out_shape=bf_out,
        scratch_shapes=[pltpu.VMEM((DN_HEADS, DN_HEAD_DIM, DN_HEAD_DIM), F32)],
        compiler_params=_params("arbitrary", "arbitrary"),
        name="dn_scan",
    )(u, w, qd, kd, qk, proj, egl, norm_w)


SWA_TILE = 2 * LANES


def _swa_consts():
    r = np.arange(SWA_TILE)[:, None]
    c = np.arange(SWA_TILE)[None, :]
    ones = (r // SWA_HEAD_DIM) == (c // SWA_HEAD_DIM)
    cl = c % SWA_HEAD_DIM
    perm = ((cl < ROPE_HALF) & (r == c + ROPE_HALF)) | ((cl >= ROPE_HALF) & (cl < ROPE_DIM) & (r == c - ROPE_HALF))
    return np.stack([ones, perm]).astype(np.float32)


def _norm_rope_tiles(tiles, nws, coss, sins, const_ref):
    dot = functools.partial(jnp.dot, preferred_element_type=F32)
    ws = [t.shape[1] for t in tiles]
    ms = [dot((t * t).astype(BF16), const_ref[0, :w, :w]) for t, w in zip(tiles, ws)]
    ys = [t * lax.rsqrt(m * (1.0 / SWA_HEAD_DIM) + EPS) * nw for t, m, nw in zip(tiles, ms, nws)]
    partners = [dot(y.astype(BF16), const_ref[1, :w, :w]) for y, w in zip(ys, ws)]
    return [y * cs[:, :w] + p * sn[:, :w] for y, p, w, cs, sn in zip(ys, partners, ws, coss, sins)]


def _swa_kernel(sink_ref, q_ref, k_ref, v_ref, cos_ref, sin_ref, qn_ref, kn_ref, const_ref, o_ref,
                kprev, vprev):
    n = pl.program_id(1)
    blk = SWA_BLOCK

    @pl.when(n == 0)
    def _():
        kprev[...] = jnp.zeros_like(kprev)
        vprev[...] = jnp.zeros_like(vprev)

    n_qt = SWA_WIDTH // SWA_TILE
    blocks = [slice(sb * blk, (sb + 1) * blk) for sb in range(SWA_STEP // blk)]
    tiles, nws, coss, sins = [], [], [], []
    for rows in blocks:
        cos_t = jnp.tile(cos_ref[rows, :], (1, SWA_TILE // LANES))
        sin_t = jnp.tile(sin_ref[rows, :], (1, SWA_TILE // LANES))
        tiles += [q_ref[rows, t * SWA_TILE:(t + 1) * SWA_TILE].astype(F32) for t in range(n_qt)]
        tiles.append(k_ref[rows, :].astype(F32))
        nws += [qn_ref[:, t * SWA_TILE:(t + 1) * SWA_TILE] for t in range(n_qt)] + [kn_ref[...]]
        coss += [cos_t] * (n_qt + 1)
        sins += [sin_t] * (n_qt + 1)
    roped = _norm_rope_tiles(tiles, nws, coss, sins, const_ref)

    for sb, rows in enumerate(blocks):
        mine = roped[sb * (n_qt + 1):(sb + 1) * (n_qt + 1)]
        qb = jnp.concatenate([(t * (SWA_HEAD_DIM ** -0.5 * LOG2E)).astype(BF16) for t in mine[:n_qt]], axis=1)
        prev_penalty = jnp.where(n > 0, 0.0, NEG_BIG) if sb == 0 else 0.0
        _swa_block(rows, prev_penalty, qb, mine[n_qt], sink_ref, v_ref, o_ref, kprev, vprev)


def _swa_block(rows, prev_penalty, qb, k_roped, sink_ref, v_ref, o_ref, kprev, vprev):
    blk = SWA_BLOCK

    low = lax.broadcasted_iota(jnp.int32, (blk, LANES), 1) < SWA_HEAD_DIM

    def split_heads(cur, prev_ref):
        swap = pltpu.roll(cur, SWA_HEAD_DIM, axis=1)
        parts = [jnp.where(low, cur, 0.0), jnp.where(low, swap, 0.0),
                 jnp.where(low, 0.0, swap), jnp.where(low, 0.0, cur)]
        bands = []
        for idx, part in enumerate(parts):
            part = part.astype(BF16)
            bands.append(jnp.concatenate([prev_ref[idx], part], axis=0))
            prev_ref[idx] = part
        return bands

    k_bands = split_heads(k_roped, kprev)
    v_bands = split_heads(v_ref[rows, :].astype(F32), vprev)

    qi = lax.broadcasted_iota(jnp.int32, (blk, 2 * blk), 0)
    kj = lax.broadcasted_iota(jnp.int32, (blk, 2 * blk), 1)
    rel = qi + blk - kj
    bias = jnp.where(rel >= 0, jnp.where(rel < blk, 0.0, NEG_BIG), NEG_BIG)
    bias = bias + jnp.where(kj < blk, prev_penalty, 0.0)

    group = SWA_Q_HEADS // SWA_KV_HEADS
    for h0 in range(0, SWA_Q_HEADS, SWA_HEAD_BATCH):
        heads = range(h0, h0 + SWA_HEAD_BATCH)
        kops = [k_bands[2 * (h % 2) + h // group] for h in heads]
        vops = [v_bands[2 * (h % 2) + h // group] for h in heads]
        sinks = [sink_ref[h] * LOG2E for h in heads]
        ss = [_mm_nt(qb[:, (h // 2) * LANES:(h // 2 + 1) * LANES], kop) + bias
              for h, kop in zip(heads, kops)]
        mxs = [jnp.maximum(jnp.max(s, axis=-1, keepdims=True), sink) for s, sink in zip(ss, sinks)]
        ps = [jnp.exp2(s - mx) for s, mx in zip(ss, mxs)]
        denoms = [jnp.sum(p, axis=-1, keepdims=True) + jnp.exp2(sink - mx)
                  for p, sink, mx in zip(ps, sinks, mxs)]
        outs = [_mm(p, vop) * (1.0 / d) for p, vop, d in zip(ps, vops, denoms)]
        for pair in range(h0 // 2, (h0 + SWA_HEAD_BATCH) // 2):
            o_ref[rows, pair * LANES:(pair + 1) * LANES] = (
                outs[2 * pair - h0] + outs[2 * pair + 1 - h0]).astype(BF16)


def _swa(proj, cos_t, sin_t, sinks, qn, kn, layer, batch, seq):
    m = proj.shape[0]
    blk, step = SWA_BLOCK, SWA_STEP
    nb = seq // step
    row = lambda b, n: b * nb + n
    consts = jnp.asarray(_swa_consts(), BF16)
    return pl.pallas_call(
        _swa_kernel,
        grid=(batch, nb),
        in_specs=[pl.BlockSpec(memory_space=pltpu.SMEM),
                  pl.BlockSpec((step, SWA_WIDTH), lambda b, n: (row(b, n), COL_SWQ // SWA_WIDTH)),
                  pl.BlockSpec((step, SWA_KV_WIDTH), lambda b, n: (row(b, n), COL_SWK // SWA_KV_WIDTH)),
                  pl.BlockSpec((step, SWA_KV_WIDTH), lambda b, n: (row(b, n), COL_SWV // SWA_KV_WIDTH)),
                  pl.BlockSpec((step, LANES), lambda b, n: (row(b, n), 0)),
                  pl.BlockSpec((step, LANES), lambda b, n: (row(b, n), 0)),
                  pl.BlockSpec((None, 1, SWA_WIDTH), lambda b, n: (layer, 0, 0)),
                  pl.BlockSpec((None, 1, SWA_KV_WIDTH), lambda b, n: (layer, 0, 0)),
                  pl.BlockSpec(consts.shape, lambda b, n: (0, 0, 0))],
        out_specs=pl.BlockSpec((step, SWA_WIDTH), lambda b, n: (row(b, n), 0)),
        out_shape=jax.ShapeDtypeStruct((m, SWA_WIDTH), BF16),
        scratch_shapes=[pltpu.VMEM((2 * SWA_KV_HEADS, blk, SWA_KV_WIDTH), BF16),
                        pltpu.VMEM((2 * SWA_KV_HEADS, blk, SWA_KV_WIDTH), BF16)],
        compiler_params=_params("arbitrary", "arbitrary"),
        name="swa",
    )(sinks, proj, proj, proj, cos_t, sin_t, qn, kn, consts)


def _merge_kernel(x_ref, odn_ref, osw_ref, ga_ref, gb_ref, gt_ref, wdn_ref, wsw_ref, wo_ref, o_ref):
    ya = jnp.dot(odn_ref[...], wdn_ref[...], preferred_element_type=F32)
    yb = jnp.dot(osw_ref[...], wsw_ref[...], preferred_element_type=F32)
    merged = _sigmoid(ga_ref[...].astype(F32)) * ya + _sigmoid(gb_ref[...].astype(F32)) * yb
    out = jnp.dot(merged.astype(BF16), wo_ref[...], preferred_element_type=F32)
    o_ref[...] = x_ref[...] + gt_ref[0] * out


def _merge(x, o_dn, o_sw, proj, mod, w_dn, w_sw, w_o, layer, seq):
    m = x.shape[0]
    tm = 1024
    tiles_per_seq = seq // tm
    tok = lambda cb: pl.BlockSpec((tm, D_MODEL), lambda i: (i, cb))
    wfull = pl.BlockSpec((None, D_MODEL, D_MODEL), lambda i: (layer, 0, 0))
    return pl.pallas_call(
        _merge_kernel,
        grid=(m // tm,),
        in_specs=[tok(0), tok(0), tok(0), tok(COL_GA // D_MODEL), tok(COL_GB // D_MODEL),
                  _mod_spec(layer, 2, tiles_per_seq), wfull, wfull, wfull],
        out_specs=tok(0),
        out_shape=jax.ShapeDtypeStruct((m, D_MODEL), F32),
        compiler_params=_params("arbitrary"),
        name="merge_out",
    )(x, o_dn, o_sw, proj, proj, mod, w_dn, w_sw, w_o)


def _ffn_kernel(x_ref, nw_ref, sc_ref, sh_ref, gt_ref, wa_ref, wl_ref, cw_ref, cb_ref, wd_ref,
                o_ref, h_scr, acc_scr, abuf, halo_scr, *, tm, tiles_per_seq, n_ff_tiles):
    i = pl.program_id(0)
    j = pl.program_id(1)
    halo = SUBLANES

    @pl.when(j == 0)
    def _():
        h_scr[...] = _norm_mod(x_ref[...], nw_ref[...], sc_ref[0], sh_ref[0]).astype(BF16)

    h = h_scr[...]
    a = jnp.dot(h, wa_ref[...], preferred_element_type=F32)
    lin = jnp.dot(h, wl_ref[...], preferred_element_type=F32)

    first = (i % tiles_per_seq) == 0
    prev = halo_scr[j]
    abuf[0:halo, :] = jnp.where(first, jnp.zeros_like(prev), prev)
    abuf[halo:halo + tm, :] = a
    halo_scr[j] = a[tm - halo:tm, :]
    w = cw_ref[...]
    y = w[FFN_CONV - 1:FFN_CONV, :] * a + cb_ref[...]
    for s in range(1, FFN_CONV):
        y = y + w[FFN_CONV - 1 - s:FFN_CONV - s, :] * abuf[halo - s:halo - s + tm, :]
    act = (_silu(y) * lin).astype(BF16)
    part = jnp.dot(act, wd_ref[...], preferred_element_type=F32)

    @pl.when(j == 0)
    def _():
        acc_scr[...] = part

    @pl.when(j > 0)
    def _():
        acc_scr[...] += part

    @pl.when(j == n_ff_tiles - 1)
    def _():
        o_ref[...] = x_ref[...] + gt_ref[0] * acc_scr[...]


def _ffn(x, nw, mod, w_up, conv_w, conv_b, w_down, layer, seq):
    m = x.shape[0]
    tm = 512
    n_ff = 2
    fc = D_FF // n_ff
    tiles_per_seq = seq // tm
    kern = functools.partial(_ffn_kernel, tm=tm, tiles_per_seq=tiles_per_seq, n_ff_tiles=n_ff)
    return pl.pallas_call(
        kern,
        grid=(m // tm, n_ff),
        in_specs=[pl.BlockSpec((tm, D_MODEL), lambda i, j: (i, 0)),
                  pl.BlockSpec((None, 1, D_MODEL), lambda i, j: (layer, 0, 0)),
                  _mod_spec(layer, 4, tiles_per_seq), _mod_spec(layer, 3, tiles_per_seq),
                  _mod_spec(layer, 5, tiles_per_seq),
                  pl.BlockSpec((None, D_MODEL, fc), lambda i, j: (layer, 0, j)),
                  pl.BlockSpec((None, D_MODEL, fc), lambda i, j: (layer, 0, n_ff + j)),
                  pl.BlockSpec((None, FFN_CONV, fc), lambda i, j: (layer, 0, j)),
                  pl.BlockSpec((None, 1, fc), lambda i, j: (layer, 0, j)),
                  pl.BlockSpec((None, fc, D_MODEL), lambda i, j: (layer, j, 0))],
        out_specs=pl.BlockSpec((tm, D_MODEL), lambda i, j: (i, 0)),
        out_shape=jax.ShapeDtypeStruct((m, D_MODEL), F32),
        scratch_shapes=[pltpu.VMEM((tm, D_MODEL), BF16),
                        pltpu.VMEM((tm, D_MODEL), F32),
                        pltpu.VMEM((tm + SUBLANES, fc), F32),
                        pltpu.VMEM((n_ff, SUBLANES, fc), F32)],
        compiler_params=_params("arbitrary", "arbitrary"),
        name="ffn",
    )(x, nw, mod, mod, mod, w_up, w_up, conv_w, conv_b, w_down)


def _pack_moves():
    o_a = 4 * DN_WIDTH
    o_swq = o_a + 2 * DN_HEADS
    o_swk = o_swq + SWA_WIDTH
    o_swv = o_swk + SWA_KV_WIDTH
    o_ga = o_swv + SWA_KV_WIDTH
    o_gb = o_ga + D_MODEL
    return ((0, COL_Q, o_a), (o_swq, COL_SWQ, SWA_WIDTH), (o_ga, COL_GA, D_MODEL), (o_gb, COL_GB, D_MODEL),
            (o_swk, COL_SWK, SWA_KV_WIDTH), (o_swv, COL_SWV, SWA_KV_WIDTH), (o_a, COL_AB, 2 * DN_HEADS))


def _pack_kernel(w_ref, o_ref):
    for src, dst, width in _pack_moves():
        o_ref[dst:dst + width, :] = w_ref[src:src + width, :].astype(BF16)
    tail = COL_AB + 2 * DN_HEADS
    o_ref[tail:, :] = jnp.zeros((IN_PACKED - tail, o_ref.shape[1]), BF16)


def _pack_w_in(w_in):
    depth, d_in, n_in = w_in.shape
    w_t = jnp.swapaxes(w_in, 1, 2)
    tc = 256
    return pl.pallas_call(
        _pack_kernel,
        grid=(depth, d_in // tc),
        in_specs=[pl.BlockSpec((None, n_in, tc), lambda l, i: (l, 0, i))],
        out_specs=pl.BlockSpec((None, IN_PACKED, tc), lambda l, i: (l, 0, i)),
        out_shape=jax.ShapeDtypeStruct((depth, IN_PACKED, d_in), BF16),
        compiler_params=_params("parallel", "parallel"),
        name="pack_w_in",
    )(w_t)


def _lane_row(v):
    depth, n = v.shape
    return jnp.zeros((depth, 1, LANES), F32).at[:, 0, :n].set(v.astype(F32))


def kernel(x, c, positions, w_ada, b_ada, norm_mix, w_in, dn_conv, dn_a_log, dn_dt_bias, dn_norm,
           w_dn_out, swa_q_norm, swa_k_norm, swa_sinks, w_swa_out, w_o, norm_ffn, w_up, ffn_conv,
           ffn_conv_b, w_down):
    batch, seq, _ = x.shape
    depth = w_ada.shape[0]
    m = batch * seq

    mod_all = _ada_mod(c, w_ada, b_ada)
    cos_t, sin_t = _rope_tables(positions)

    w_in_p = _pack_w_in(w_in)
    w_dn_b, w_sw_b, w_o_b = w_dn_out.astype(BF16), w_swa_out.astype(BF16), w_o.astype(BF16)
    w_up_b, w_down_b = w_up.astype(BF16), w_down.astype(BF16)
    alog = _lane_row(dn_a_log)
    dtb = _lane_row(dn_dt_bias)
    qn = jnp.tile(swa_q_norm, (1, SWA_Q_HEADS)).reshape(depth, 1, SWA_WIDTH)
    kn = jnp.tile(swa_k_norm, (1, SWA_KV_HEADS)).reshape(depth, 1, SWA_KV_WIDTH)

    mod = mod_all.reshape(depth * SUBLANES * 6, 1, D_MODEL)
    norm_mix3 = norm_mix.reshape(depth, 1, D_MODEL)
    norm_ffn3 = norm_ffn.reshape(depth, 1, D_MODEL)
    dn_norm3 = dn_norm.reshape(depth, 1, DN_HEAD_DIM)
    conv_b3 = ffn_conv_b.reshape(depth, 1, D_FF)

    xf = x.reshape(m, D_MODEL)
    for l in range(depth):
        proj, gates = _inproj(xf, norm_mix3, mod, w_in_p, alog, dtb, l, seq)
        o_dn = _deltanet(proj, gates, dn_conv, dn_norm3, l, batch, seq)
        o_sw = _swa(proj, cos_t, sin_t, swa_sinks[l], qn, kn, l, batch, seq)
        xf = _merge(xf, o_dn, o_sw, proj, mod, w_dn_b, w_sw_b, w_o_b, l, seq)
        xf = _ffn(xf, norm_ffn3, mod, w_up_b, ffn_conv, conv_b3, w_down_b, l, seq)
    return xf.reshape(batch, seq, D_MODEL)
```

```python
import functools

import numpy as np
import jax
import jax.numpy as jnp
from jax import lax
from jax.experimental import pallas as pl
from jax.experimental.pallas import tpu as pltpu

F32 = jnp.float32
BF16 = jnp.bfloat16

D_MODEL = 1024
DN_HEADS = 8
DN_HEAD_DIM = 128
DN_WIDTH = DN_HEADS * DN_HEAD_DIM
DN_CONV = 4
DN_CHUNK = 64
SWA_Q_HEADS = 16
SWA_KV_HEADS = 2
SWA_HEAD_DIM = 64
SWA_WIDTH = SWA_Q_HEADS * SWA_HEAD_DIM
SWA_KV_WIDTH = SWA_KV_HEADS * SWA_HEAD_DIM
SWA_BLOCK = 128
SWA_STEP = 512
SWA_HEAD_BATCH = 16
ROPE_THETA = 500000.0
ROPE_DIM = SWA_HEAD_DIM // 4
ROPE_HALF = ROPE_DIM // 2
D_FF = 2816
FFN_CONV = 3
EPS = 1e-6

LANES = 128
SUBLANES = 8
VMEM_LIMIT = 56 * 1024 * 1024

COL_Q, COL_K, COL_V, COL_Z = 0, 1024, 2048, 3072
COL_SWQ, COL_GA, COL_GB = 4096, 5120, 6144
COL_SWK, COL_SWV, COL_AB = 7168, 7296, 7424
IN_PACKED = 7680
NEG_BIG = -1e30


def _sigmoid(x):
    return 1.0 / (1.0 + jnp.exp(-x))


def _silu(x):
    return x * _sigmoid(x)


def _mm(a, b):
    return jnp.dot(a.astype(BF16), b.astype(BF16), preferred_element_type=F32)


def _mm_nt(a, b):
    return lax.dot_general(a.astype(BF16), b.astype(BF16), (((1,), (1,)), ((), ())),
                           preferred_element_type=F32)


def _params(*sem):
    return pltpu.CompilerParams(dimension_semantics=sem, vmem_limit_bytes=VMEM_LIMIT)


def _ada_kernel(c_ref, w_ref, b_ref, o_ref):
    ca = _silu(c_ref[...])
    o_ref[0] = jnp.dot(ca, w_ref[0], precision=lax.Precision.HIGHEST,
                       preferred_element_type=F32) + b_ref[0]


def _ada_mod(c, w_ada, b_ada):
    depth = w_ada.shape[0]
    batch = c.shape[0]
    n_out = w_ada.shape[2]
    tn = 1536
    c_pad = jnp.zeros((SUBLANES, D_MODEL), F32).at[:batch].set(c)
    return pl.pallas_call(
        _ada_kernel,
        grid=(depth, n_out // tn),
        in_specs=[pl.BlockSpec((SUBLANES, D_MODEL), lambda l, j: (0, 0)),
                  pl.BlockSpec((1, D_MODEL, tn), lambda l, j: (l, 0, j)),
                  pl.BlockSpec((1, 1, tn), lambda l, j: (l, 0, j))],
        out_specs=pl.BlockSpec((1, SUBLANES, tn), lambda l, j: (l, 0, j)),
        out_shape=jax.ShapeDtypeStruct((depth, SUBLANES, n_out), F32),
        compiler_params=_params("arbitrary", "arbitrary"),
        name="ada_mod",
    )(c_pad, w_ada, b_ada.reshape(depth, 1, n_out))


def _rope_kernel(pos_ref, inv_ref, sgn_ref, cos_ref, sin_ref):
    ang = pos_ref[...] * inv_ref[...]
    on = sgn_ref[...] != 0.0
    cos_ref[...] = jnp.where(on, jnp.cos(ang), 1.0)
    sin_ref[...] = jnp.sin(ang) * sgn_ref[...]


def _rope_tables(positions):
    m = positions.size
    tm = min(2048, m)
    lane = np.arange(LANES) % SWA_HEAD_DIM
    inv = np.where(lane < ROPE_DIM,
                   np.power(ROPE_THETA, -(lane % ROPE_HALF).astype(np.float64) / ROPE_HALF), 0.0)
    sgn = np.where(lane < ROPE_HALF, -1.0, np.where(lane < ROPE_DIM, 1.0, 0.0))
    pos = positions.astype(F32).reshape(m, 1)
    return pl.pallas_call(
        _rope_kernel,
        grid=(m // tm,),
        in_specs=[pl.BlockSpec((tm, 1), lambda i: (i, 0)),
                  pl.BlockSpec((1, LANES), lambda i: (0, 0)),
                  pl.BlockSpec((1, LANES), lambda i: (0, 0))],
        out_specs=[pl.BlockSpec((tm, LANES), lambda i: (i, 0)),
                   pl.BlockSpec((tm, LANES), lambda i: (i, 0))],
        out_shape=[jax.ShapeDtypeStruct((m, LANES), F32)] * 2,
        compiler_params=_params("arbitrary"),
        name="rope_tables",
    )(pos, jnp.asarray(inv, F32).reshape(1, LANES), jnp.asarray(sgn, F32).reshape(1, LANES))


def _norm_mod(x, nw, sc, sh):
    ms = jnp.mean(x * x, axis=-1, keepdims=True)
    return (x * lax.rsqrt(ms + EPS) * nw) * (1.0 + sc) + sh


def _inproj_kernel(x_ref, nw_ref, sc_ref, sh_ref, w_ref, alog_ref, dtb_ref,
                   proj_ref, gate_ref, h_scr, *, n_col_tiles, ab_off):
    j = pl.program_id(1)

    @pl.when(j == 0)
    def _():
        h_scr[...] = _norm_mod(x_ref[...], nw_ref[...], sc_ref[0], sh_ref[0]).astype(BF16)

    acc = _mm_nt(h_scr[...], w_ref[...])
    proj_ref[...] = acc.astype(BF16)

    @pl.when(j == n_col_tiles - 1)
    def _():
        ab = acc[:, ab_off:ab_off + LANES]
        z = ab + dtb_ref[...]
        softplus = jnp.maximum(z, 0.0) + jnp.log(1.0 + jnp.exp(-jnp.abs(z)))
        g = -jnp.exp(alog_ref[...]) * softplus
        lane = lax.broadcasted_iota(jnp.int32, ab.shape, 1)
        gate_ref[...] = jnp.where(lane < DN_HEADS, g, _sigmoid(ab))


def _mod_spec(layer, k, tiles_per_seq):
    return pl.BlockSpec((1, 1, D_MODEL),
                        lambda i, *_: ((layer * SUBLANES + i // tiles_per_seq) * 6 + k, 0, 0))


def _inproj(x, nw, mod, w, alog, dtb, layer, seq):
    m = x.shape[0]
    tm, tn = 1024, 2560
    nj = IN_PACKED // tn
    tiles_per_seq = seq // tm
    kern = functools.partial(_inproj_kernel, n_col_tiles=nj, ab_off=COL_AB - (nj - 1) * tn)
    return pl.pallas_call(
        kern,
        grid=(m // tm, nj),
        in_specs=[pl.BlockSpec((tm, D_MODEL), lambda i, j: (i, 0)),
                  pl.BlockSpec((None, 1, D_MODEL), lambda i, j: (layer, 0, 0)),
                  _mod_spec(layer, 1, tiles_per_seq), _mod_spec(layer, 0, tiles_per_seq),
                  pl.BlockSpec((None, tn, D_MODEL), lambda i, j: (layer, j, 0)),
                  pl.BlockSpec((None, 1, LANES), lambda i, j: (layer, 0, 0)),
                  pl.BlockSpec((None, 1, LANES), lambda i, j: (layer, 0, 0))],
        out_specs=[pl.BlockSpec((tm, tn), lambda i, j: (i, j)),
                   pl.BlockSpec((tm, LANES), lambda i, j: (i, 0))],
        out_shape=[jax.ShapeDtypeStruct((m, IN_PACKED), BF16),
                   jax.ShapeDtypeStruct((m, LANES), F32)],
        scratch_shapes=[pltpu.VMEM((tm, D_MODEL), BF16)],
        compiler_params=_params("arbitrary", "arbitrary"),
        name="inproj",
    )(x, nw, mod, mod, w, alog, dtb)


DN_BLOCK = 512
DN_PASS_UNITS = 2
DN_UNIT = 128
INV_BLOCK = 16
DN_PREV_ROWS = 16
M_INCL, M_STRICT, M_DIAG, M_EYE, M_NEG_INCL, M_OFF0 = 0, 1, 2, 3, 4, 5
LOG2E = float(np.log2(np.e))


def _dn_masks():
    r = np.arange(DN_UNIT)[:, None]
    c = np.arange(DN_UNIT)[None, :]
    same = lambda b: (r // b) == (c // b)
    chunk = same(DN_CHUNK)
    incl = chunk & (r >= c)
    masks = [incl, chunk & (r > c), same(INV_BLOCK), r == c, np.where(incl, 0.0, NEG_BIG)]
    b = INV_BLOCK
    while b < DN_CHUNK:
        masks.append(same(2 * b) & ~same(b))
        b *= 2
    return np.stack(masks).astype(np.float32)


def _inv_unit_lower(l_strict, mask_ref, filler):
    dot = functools.partial(jnp.dot, preferred_element_type=F32)
    ds = [l * mask_ref[M_DIAG] for l in l_strict]
    ts = [mask_ref[M_EYE] - d for d in ds]
    dbs = [d.astype(BF16) for d in ds]
    mpows = [dot(db, db) for db in dbs]
    filler()
    n_fac = int(np.log2(INV_BLOCK)) - 1
    for i in range(n_fac):
        mbs = [m.astype(BF16) for m in mpows]
        ts = [t + dot(t.astype(BF16), mb) for t, mb in zip(ts, mbs)]
        if i < n_fac - 1:
            mpows = [dot(mb, mb) for mb in mbs]
        filler()
    n_levels = int(np.log2(DN_CHUNK // INV_BLOCK))
    for lvl in range(n_levels):
        tbs = [t.astype(BF16) for t in ts]
        inner = [dot((l * mask_ref[M_OFF0 + lvl]).astype(BF16), tb).astype(BF16)
                 for l, tb in zip(l_strict, tbs)]
        filler()
        ts = [t - dot(tb, inn) for t, tb, inn in zip(ts, tbs, inner)]
        filler()
    return ts


def _dn_conv_pieces(raw_refs, prev_refs, keep_prev, cw_ref, xbuf, act_ref):
    tb, halo = DN_BLOCK, DN_PREV_ROWS

    def piece(idx, h):
        sl = slice(h * DN_HEAD_DIM, (h + 1) * DN_HEAD_DIM)
        ref, pref = raw_refs[idx], prev_refs[idx]
        if pref is None:
            xbuf[idx, 0:halo, sl] = jnp.zeros((halo, DN_HEAD_DIM), F32)
        else:
            xbuf[idx, 0:halo, sl] = pref[:, sl].astype(F32) * keep_prev
        x = ref[:, sl].astype(F32)
        xbuf[idx, halo:halo + tb, sl] = x
        w = cw_ref[:, idx * DN_WIDTH + h * DN_HEAD_DIM:idx * DN_WIDTH + (h + 1) * DN_HEAD_DIM]
        y = w[DN_CONV - 1:DN_CONV, :] * x
        for s in range(1, DN_CONV):
            y = y + w[DN_CONV - 1 - s:DN_CONV - s, :] * xbuf[idx, halo - s:halo - s + tb, sl]
        y = _silu(y)
        if idx < 2:
            scale = DN_HEAD_DIM ** -0.5 if idx == 0 else 1.0
            y = y * (lax.rsqrt(jnp.sum(y * y, axis=-1, keepdims=True) + EPS) * scale)
        act_ref[idx, :, sl] = y

    return [functools.partial(piece, idx, h) for idx in range(3) for h in range(DN_HEADS)]


def _dn_intra_kernel(q_ref, k_ref, v_ref, qp_ref, kp_ref, vp_ref, gate_ref, cw_ref, mask_ref,
                     u_ref, w_ref, qd_ref, kd_ref, qk_ref, egl_ref, xbuf, act_cur, *, blocks_per_seq):
    i = pl.program_id(0)
    tb, c = DN_BLOCK, DN_CHUNK
    n_chunks = tb // c

    keep_prev = jnp.where((i % blocks_per_seq) == 0, 0.0, 1.0)
    for piece in _dn_conv_pieces((q_ref, k_ref, v_ref), (qp_ref, kp_ref, vp_ref), keep_prev,
                                 cw_ref, xbuf, act_cur):
        piece()
    pending = []

    un = DN_UNIT
    units = [slice(p * un, (p + 1) * un) for p in range(tb // un)]
    gates = gate_ref[...]
    gcum = jnp.concatenate(
        [jnp.dot(mask_ref[M_INCL], gates[rows], precision=lax.Precision.HIGHEST, preferred_element_type=F32)
         for rows in units], axis=0)
    gcum_t = gcum.T
    gcum2 = gcum * LOG2E
    gcum2_t = gcum_t * LOG2E
    glast = jnp.concatenate(
        [jnp.broadcast_to(gcum[ci * c + c - 1:ci * c + c, :], (c, LANES)) for ci in range(n_chunks)], axis=0)
    e_cum = jnp.exp(gcum)
    e_rem = jnp.exp(glast - gcum)
    for ci in range(n_chunks):
        gl = gcum_t[0:DN_HEADS, ci * c + c - 1:ci * c + c]
        egl_ref[ci * DN_HEADS:(ci + 1) * DN_HEADS, :] = jnp.broadcast_to(jnp.exp(gl), (DN_HEADS, LANES))

    def head_setup(h, part, rhs, ls):
        sl = slice(h * DN_HEAD_DIM, (h + 1) * DN_HEAD_DIM)
        span = slice(part[0].start, part[-1].stop)
        qh, kh, vh = act_cur[0, span, sl], act_cur[1, span, sl], act_cur[2, span, sl]
        beta = jnp.broadcast_to(gates[span, DN_HEADS + h:DN_HEADS + h + 1], kh.shape)
        eg = jnp.broadcast_to(e_cum[span, h:h + 1], kh.shape)
        kb = kh * beta
        qd_ref[span, sl] = (qh * eg).astype(BF16)
        kd_ref[span, sl] = (kh * e_rem[span, h:h + 1]).astype(BF16)
        rhs_h = jnp.concatenate([vh * beta, kb * eg], axis=1).astype(BF16)
        for rows in part:
            loc = slice(rows.start - span.start, rows.stop - span.start)
            rhs.append(rhs_h[loc])
            a = _mm_nt(jnp.concatenate([kb[loc], qh[loc]], axis=0), kh[loc])
            decay = jnp.exp2(gcum2[rows, h:h + 1] - gcum2_t[h:h + 1, rows] + mask_ref[M_NEG_INCL])
            ls.append(a[:un] * (decay * mask_ref[M_STRICT]))
            qk_ref[rows, sl] = (a[un:] * decay).astype(BF16)

    def filler():
        if pending:
            pending.pop(0)()

    heads = range(DN_HEADS)
    for u0 in range(0, len(units), DN_PASS_UNITS):
        part = units[u0:u0 + DN_PASS_UNITS]
        rhs, ls = [], []
        for h in heads:
            head_setup(h, part, rhs, ls)
        tinvs = _inv_unit_lower(ls, mask_ref, filler)
        where = [(h, rows) for h in heads for rows in part]
        for (h, rows), tinv, r in zip(where, tinvs, rhs):
            sl = slice(h * DN_HEAD_DIM, (h + 1) * DN_HEAD_DIM)
            uw = jnp.dot(tinv.astype(BF16), r, preferred_element_type=F32)
            u_ref[rows, sl] = uw[:, :DN_HEAD_DIM]
            w_ref[rows, sl] = uw[:, DN_HEAD_DIM:].astype(BF16)


def _dn_scan_kernel(u_ref, w_ref, qd_ref, kd_ref, qk_ref, z_ref, egl_ref, nw_ref, o_ref, s_scr, *, tb):
    t = pl.program_id(1)
    c = DN_CHUNK

    @pl.when(t == 0)
    def _():
        s_scr[...] = jnp.zeros_like(s_scr)

    nw = nw_ref[...]
    zeros_v = jnp.zeros((c, DN_HEAD_DIM), BF16)
    heads = range(DN_HEADS)
    sls = [slice(h * DN_HEAD_DIM, (h + 1) * DN_HEAD_DIM) for h in heads]
    dot = functools.partial(jnp.dot, preferred_element_type=F32)

    def out_matmuls(rows, ws, pads):
        return [w_s[c:] + dot(qk_ref[rows, sl], v_pad) for sl, w_s, v_pad in zip(sls, ws, pads)]

    def out_finish(rows, outs):
        for sl, o in zip(sls, outs):
            o = o * lax.rsqrt(jnp.mean(o * o, axis=-1, keepdims=True) + EPS) * nw
            o_ref[rows, sl] = (o * _silu(z_ref[rows, sl].astype(F32))).astype(BF16)

    pending = None
    for ci in range(tb // c):
        rows = slice(ci * c, (ci + 1) * c)
        s_old = [s_scr[h] for h in heads]
        ws = [dot(jnp.concatenate([w_ref[rows, sl], qd_ref[rows, sl]], axis=0), s.astype(BF16))
              for sl, s in zip(sls, s_old)]
        prev_outs = out_matmuls(*pending) if pending else None
        vbs = [(u_ref[rows, sl] - w_s[:c]).astype(BF16) for sl, w_s in zip(sls, ws)]
        pads = [jnp.concatenate([vb, zeros_v] if ci % 2 == 0 else [zeros_v, vb], axis=0) for vb in vbs]
        kd_ts = [kd_ref[rows, sl].astype(F32).T.astype(BF16) for sl in sls]
        for h, s, vb, kd_t in zip(heads, s_old, vbs, kd_ts):
            egl = egl_ref[ci * DN_HEADS + h:ci * DN_HEADS + h + 1, :]
            s_scr[h] = s * egl + dot(kd_t, vb)
        if pending:
            out_finish(pending[0], prev_outs)
        pending = (rows, ws, pads)
    out_finish(pending[0], out_matmuls(*pending))


def _deltanet(proj, gates, conv_w, norm_w, layer, batch, seq):
    m = proj.shape[0]
    tb = DN_BLOCK
    nblk = m // tb
    masks = jnp.asarray(_dn_masks())
    prev_per_blk = tb // DN_PREV_ROWS
    cols = (COL_Q // DN_WIDTH, COL_K // DN_WIDTH, COL_V // DN_WIDTH)
    cur = [pl.BlockSpec((tb, DN_WIDTH), lambda i, cb=cb: (i, cb)) for cb in cols]
    prev = [pl.BlockSpec((DN_PREV_ROWS, DN_WIDTH),
                         lambda i, cb=cb: (jnp.maximum(i * prev_per_blk - 1, 0), cb)) for cb in cols]
    tok = pl.BlockSpec((tb, DN_WIDTH), lambda i: (i, 0))
    egl_rows = (tb // DN_CHUNK) * DN_HEADS
    bf_out = jax.ShapeDtypeStruct((m, DN_WIDTH), BF16)
    u, w, qd, kd, qk, egl = pl.pallas_call(
        functools.partial(_dn_intra_kernel, blocks_per_seq=seq // tb),
        grid=(nblk,),
        in_specs=cur + prev + [
            pl.BlockSpec((tb, LANES), lambda i: (i, 0)),
            pl.BlockSpec((None, DN_CONV, 3 * DN_WIDTH), lambda i: (layer, 0, 0)),
            pl.BlockSpec(masks.shape, lambda i: (0, 0, 0))],
        out_specs=[tok, tok, tok, tok, tok, pl.BlockSpec((egl_rows, LANES), lambda i: (i, 0))],
        out_shape=[jax.ShapeDtypeStruct((m, DN_WIDTH), F32), bf_out, bf_out, bf_out, bf_out,
                   jax.ShapeDtypeStruct((m // DN_CHUNK * DN_HEADS, LANES), F32)],
        scratch_shapes=[pltpu.VMEM((3, tb + DN_PREV_ROWS, DN_WIDTH), F32),
                        pltpu.VMEM((3, tb, DN_WIDTH), F32)],
        compiler_params=_params("parallel"),
        name="dn_intra",
    )(*([proj] * 6), gates, conv_w, masks)

    ts = 1024
    nt = seq // ts
    blk = lambda cb: pl.BlockSpec((ts, DN_WIDTH), lambda b, t: (b * nt + t, cb))
    return pl.pallas_call(
        functools.partial(_dn_scan_kernel, tb=ts),
        grid=(batch, nt),
        in_specs=[blk(0), blk(0), blk(0), blk(0), blk(0), blk(COL_Z // DN_WIDTH),
                  pl.BlockSpec((ts // DN_CHUNK * DN_HEADS, LANES), lambda b, t: (b * nt + t, 0)),
                  pl.BlockSpec((None, 1, DN_HEAD_DIM), lambda b, t: (layer, 0, 0))],
        out_specs=blk(0),
        out_shape=bf_out,
        scratch_shapes=[pltpu.VMEM((DN_HEADS, DN_HEAD_DIM, DN_HEAD_DIM), F32)],
        compiler_params=_params("arbitrary", "arbitrary"),
        name="dn_scan",
    )(u, w, qd, kd, qk, proj, egl, norm_w)


SWA_TILE = 2 * LANES


def _swa_consts():
    r = np.arange(SWA_TILE)[:, None]
    c = np.arange(SWA_TILE)[None, :]
    ones = (r // SWA_HEAD_DIM) == (c // SWA_HEAD_DIM)
    cl = c % SWA_HEAD_DIM
    perm = ((cl < ROPE_HALF) & (r == c + ROPE_HALF)) | ((cl >= ROPE_HALF) & (cl < ROPE_DIM) & (r == c - ROPE_HALF))
    return np.stack([ones, perm]).astype(np.float32)


def _norm_rope_tiles(tiles, nws, coss, sins, const_ref):
    dot = functools.partial(jnp.dot, preferred_element_type=F32)
    ws = [t.shape[1] for t in tiles]
    ms = [dot((t * t).astype(BF16), const_ref[0, :w, :w]) for t, w in zip(tiles, ws)]
    ys = [t * lax.rsqrt(m * (1.0 / SWA_HEAD_DIM) + EPS) * nw for t, m, nw in zip(tiles, ms, nws)]
    partners = [dot(y.astype(BF16), const_ref[1, :w, :w]) for y, w in zip(ys, ws)]
    return [y * cs[:, :w] + p * sn[:, :w] for y, p, w, cs, sn in zip(ys, partners, ws, coss, sins)]


def _swa_kernel(sink_ref, q_ref, k_ref, v_ref, cos_ref, sin_ref, qn_ref, kn_ref, const_ref, o_ref,
                kprev, vprev):
    n = pl.program_id(1)
    blk = SWA_BLOCK

    @pl.when(n == 0)
    def _():
        kprev[...] = jnp.zeros_like(kprev)
        vprev[...] = jnp.zeros_like(vprev)

    n_qt = SWA_WIDTH // SWA_TILE
    blocks = [slice(sb * blk, (sb + 1) * blk) for sb in range(SWA_STEP // blk)]
    tiles, nws, coss, sins = [], [], [], []
    for rows in blocks:
        cos_t = jnp.tile(cos_ref[rows, :], (1, SWA_TILE // LANES))
        sin_t = jnp.tile(sin_ref[rows, :], (1, SWA_TILE // LANES))
        tiles += [q_ref[rows, t * SWA_TILE:(t + 1) * SWA_TILE].astype(F32) for t in range(n_qt)]
        tiles.append(k_ref[rows, :].astype(F32))
        nws += [qn_ref[:, t * SWA_TILE:(t + 1) * SWA_TILE] for t in range(n_qt)] + [kn_ref[...]]
        coss += [cos_t] * (n_qt + 1)
        sins += [sin_t] * (n_qt + 1)
    roped = _norm_rope_tiles(tiles, nws, coss, sins, const_ref)

    for sb, rows in enumerate(blocks):
        mine = roped[sb * (n_qt + 1):(sb + 1) * (n_qt + 1)]
        qb = jnp.concatenate([(t * (SWA_HEAD_DIM ** -0.5 * LOG2E)).astype(BF16) for t in mine[:n_qt]], axis=1)
        prev_penalty = jnp.where(n > 0, 0.0, NEG_BIG) if sb == 0 else 0.0
        _swa_block(rows, prev_penalty, qb, mine[n_qt], sink_ref, v_ref, o_ref, kprev, vprev)


def _swa_block(rows, prev_penalty, qb, k_roped, sink_ref, v_ref, o_ref, kprev, vprev):
    blk = SWA_BLOCK

    low = lax.broadcasted_iota(jnp.int32, (blk, LANES), 1) < SWA_HEAD_DIM

    def split_heads(cur, prev_ref):
        swap = pltpu.roll(cur, SWA_HEAD_DIM, axis=1)
        parts = [jnp.where(low, cur, 0.0), jnp.where(low, swap, 0.0),
                 jnp.where(low, 0.0, swap), jnp.where(low, 0.0, cur)]
        bands = []
        for idx, part in enumerate(parts):
            part = part.astype(BF16)
            bands.append(jnp.concatenate([prev_ref[idx], part], axis=0))
            prev_ref[idx] = part
        return bands

    k_bands = split_heads(k_roped, kprev)
    v_bands = split_heads(v_ref[rows, :].astype(F32), vprev)

    qi = lax.broadcasted_iota(jnp.int32, (blk, 2 * blk), 0)
    kj = lax.broadcasted_iota(jnp.int32, (blk, 2 * blk), 1)
    rel = qi + blk - kj
    bias = jnp.where(rel >= 0, jnp.where(rel < blk, 0.0, NEG_BIG), NEG_BIG)
    bias = bias + jnp.where(kj < blk, prev_penalty, 0.0)

    group = SWA_Q_HEADS // SWA_KV_HEADS
    for h0 in range(0, SWA_Q_HEADS, SWA_HEAD_BATCH):
        heads = range(h0, h0 + SWA_HEAD_BATCH)
        kops = [k_bands[2 * (h % 2) + h // group] for h in heads]
        vops = [v_bands[2 * (h % 2) + h // group] for h in heads]
        sinks = [sink_ref[h] * LOG2E for h in heads]
        ss = [_mm_nt(qb[:, (h // 2) * LANES:(h // 2 + 1) * LANES], kop) + bias
              for h, kop in zip(heads, kops)]
        mxs = [jnp.maximum(jnp.max(s, axis=-1, keepdims=True), sink) for s, sink in zip(ss, sinks)]
        ps = [jnp.exp2(s - mx) for s, mx in zip(ss, mxs)]
        denoms = [jnp.sum(p, axis=-1, keepdims=True) + jnp.exp2(sink - mx)
                  for p, sink, mx in zip(ps, sinks, mxs)]
        outs = [_mm(p, vop) * (1.0 / d) for p, vop, d in zip(ps, vops, denoms)]
        for pair in range(h0 // 2, (h0 + SWA_HEAD_BATCH) // 2):
            o_ref[rows, pair * LANES:(pair + 1) * LANES] = (
                outs[2 * pair - h0] + outs[2 * pair + 1 - h0]).astype(BF16)


def _swa(proj, cos_t, sin_t, sinks, qn, kn, layer, batch, seq):
    m = proj.shape[0]
    blk, step = SWA_BLOCK, SWA_STEP
    nb = seq // step
    row = lambda b, n: b * nb + n
    consts = jnp.asarray(_swa_consts(), BF16)
    return pl.pallas_call(
        _swa_kernel,
        grid=(batch, nb),
        in_specs=[pl.BlockSpec(memory_space=pltpu.SMEM),
                  pl.BlockSpec((step, SWA_WIDTH), lambda b, n: (row(b, n), COL_SWQ // SWA_WIDTH)),
                  pl.BlockSpec((step, SWA_KV_WIDTH), lambda b, n: (row(b, n), COL_SWK // SWA_KV_WIDTH)),
                  pl.BlockSpec((step, SWA_KV_WIDTH), lambda b, n: (row(b, n), COL_SWV // SWA_KV_WIDTH)),
                  pl.BlockSpec((step, LANES), lambda b, n: (row(b, n), 0)),
                  pl.BlockSpec((step, LANES), lambda b, n: (row(b, n), 0)),
                  pl.BlockSpec((None, 1, SWA_WIDTH), lambda b, n: (layer, 0, 0)),
                  pl.BlockSpec((None, 1, SWA_KV_WIDTH), lambda b, n: (layer, 0, 0)),
                  pl.BlockSpec(consts.shape, lambda b, n: (0, 0, 0))],
        out_specs=pl.BlockSpec((step, SWA_WIDTH), lambda b, n: (row(b, n), 0)),
        out_shape=jax.ShapeDtypeStruct((m, SWA_WIDTH), BF16),
        scratch_shapes=[pltpu.VMEM((2 * SWA_KV_HEADS, blk, SWA_KV_WIDTH), BF16),
                        pltpu.VMEM((2 * SWA_KV_HEADS, blk, SWA_KV_WIDTH), BF16)],
        compiler_params=_params("arbitrary", "arbitrary"),
        name="swa",
    )(sinks, proj, proj, proj, cos_t, sin_t, qn, kn, consts)


def _merge_kernel(x_ref, odn_ref, osw_ref, ga_ref, gb_ref, gt_ref, wdn_ref, wsw_ref, wo_ref, o_ref):
    ya = jnp.dot(odn_ref[...], wdn_ref[...], preferred_element_type=F32)
    yb = jnp.dot(osw_ref[...], wsw_ref[...], preferred_element_type=F32)
    merged = _sigmoid(ga_ref[...].astype(F32)) * ya + _sigmoid(gb_ref[...].astype(F32)) * yb
    out = jnp.dot(merged.astype(BF16), wo_ref[...], preferred_element_type=F32)
    o_ref[...] = x_ref[...] + gt_ref[0] * out


def _merge(x, o_dn, o_sw, proj, mod, w_dn, w_sw, w_o, layer, seq):
    m = x.shape[0]
    tm = 1024
    tiles_per_seq = seq // tm
    tok = lambda cb: pl.BlockSpec((tm, D_MODEL), lambda i: (i, cb))
    wfull = pl.BlockSpec((None, D_MODEL, D_MODEL), lambda i: (layer, 0, 0))
    return pl.pallas_call(
        _merge_kernel,
        grid=(m // tm,),
        in_specs=[tok(0), tok(0), tok(0), tok(COL_GA // D_MODEL), tok(COL_GB // D_MODEL),
                  _mod_spec(layer, 2, tiles_per_seq), wfull, wfull, wfull],
        out_specs=tok(0),
        out_shape=jax.ShapeDtypeStruct((m, D_MODEL), F32),
        compiler_params=_params("arbitrary"),
        name="merge_out",
    )(x, o_dn, o_sw, proj, proj, mod, w_dn, w_sw, w_o)


def _ffn_kernel(x_ref, nw_ref, sc_ref, sh_ref, gt_ref, wa_ref, wl_ref, cw_ref, cb_ref, wd_ref,
                o_ref, h_scr, acc_scr, abuf, halo_scr, *, tm, tiles_per_seq, n_ff_tiles):
    i = pl.program_id(0)
    j = pl.program_id(1)
    halo = SUBLANES

    @pl.when(j == 0)
    def _():
        h_scr[...] = _norm_mod(x_ref[...], nw_ref[...], sc_ref[0], sh_ref[0]).astype(BF16)

    h = h_scr[...]
    a = jnp.dot(h, wa_ref[...], preferred_element_type=F32)
    lin = jnp.dot(h, wl_ref[...], preferred_element_type=F32)

    first = (i % tiles_per_seq) == 0
    prev = halo_scr[j]
    abuf[0:halo, :] = jnp.where(first, jnp.zeros_like(prev), prev)
    abuf[halo:halo + tm, :] = a
    halo_scr[j] = a[tm - halo:tm, :]
    w = cw_ref[...]
    y = w[FFN_CONV - 1:FFN_CONV, :] * a + cb_ref[...]
    for s in range(1, FFN_CONV):
        y = y + w[FFN_CONV - 1 - s:FFN_CONV - s, :] * abuf[halo - s:halo - s + tm, :]
    act = (_silu(y) * lin).astype(BF16)
    part = jnp.dot(act, wd_ref[...], preferred_element_type=F32)

    @pl.when(j == 0)
    def _():
        acc_scr[...] = part

    @pl.when(j > 0)
    def _():
        acc_scr[...] += part

    @pl.when(j == n_ff_tiles - 1)
    def _():
        o_ref[...] = x_ref[...] + gt_ref[0] * acc_scr[...]


def _ffn(x, nw, mod, w_up, conv_w, conv_b, w_down, layer, seq):
    m = x.shape[0]
    tm = 1024
    n_ff = 2
    fc = D_FF // n_ff
    tiles_per_seq = seq // tm
    kern = functools.partial(_ffn_kernel, tm=tm, tiles_per_seq=tiles_per_seq, n_ff_tiles=n_ff)
    return pl.pallas_call(
        kern,
        grid=(m // tm, n_ff),
        in_specs=[pl.BlockSpec((tm, D_MODEL), lambda i, j: (i, 0)),
                  pl.BlockSpec((None, 1, D_MODEL), lambda i, j: (layer, 0, 0)),
                  _mod_spec(layer, 4, tiles_per_seq), _mod_spec(layer, 3, tiles_per_seq),
                  _mod_spec(layer, 5, tiles_per_seq),
                  pl.BlockSpec((None, D_MODEL, fc), lambda i, j: (layer, 0, j)),
                  pl.BlockSpec((None, D_MODEL, fc), lambda i, j: (layer, 0, n_ff + j)),
                  pl.BlockSpec((None, FFN_CONV, fc), lambda i, j: (layer, 0, j)),
                  pl.BlockSpec((None, 1, fc), lambda i, j: (layer, 0, j)),
                  pl.BlockSpec((None, fc, D_MODEL), lambda i, j: (layer, j, 0))],
        out_specs=pl.BlockSpec((tm, D_MODEL), lambda i, j: (i, 0)),
        out_shape=jax.ShapeDtypeStruct((m, D_MODEL), F32),
        scratch_shapes=[pltpu.VMEM((tm, D_MODEL), BF16),
                        pltpu.VMEM((tm, D_MODEL), F32),
                        pltpu.VMEM((tm + SUBLANES, fc), F32),
                        pltpu.VMEM((n_ff, SUBLANES, fc), F32)],
        compiler_params=_params("arbitrary", "arbitrary"),
        name="ffn",
    )(x, nw, mod, mod, mod, w_up, w_up, conv_w, conv_b, w_down)


def _pack_moves():
    o_a = 4 * DN_WIDTH
    o_swq = o_a + 2 * DN_HEADS
    o_swk = o_swq + SWA_WIDTH
    o_swv = o_swk + SWA_KV_WIDTH
    o_ga = o_swv + SWA_KV_WIDTH
    o_gb = o_ga + D_MODEL
    return ((0, COL_Q, o_a), (o_swq, COL_SWQ, SWA_WIDTH), (o_ga, COL_GA, D_MODEL), (o_gb, COL_GB, D_MODEL),
            (o_swk, COL_SWK, SWA_KV_WIDTH), (o_swv, COL_SWV, SWA_KV_WIDTH), (o_a, COL_AB, 2 * DN_HEADS))


def _pack_kernel(w_ref, o_ref):
    for src, dst, width in _pack_moves():
        o_ref[dst:dst + width, :] = w_ref[src:src + width, :].astype(BF16)
    tail = COL_AB + 2 * DN_HEADS
    o_ref[tail:, :] = jnp.zeros((IN_PACKED - tail, o_ref.shape[1]), BF16)


def _pack_w_in(w_in):
    depth, d_in, n_in = w_in.shape
    w_t = jnp.swapaxes(w_in, 1, 2)
    tc = 256
    return pl.pallas_call(
        _pack_kernel,
        grid=(depth, d_in // tc),
        in_specs=[pl.BlockSpec((None, n_in, tc), lambda l, i: (l, 0, i))],
        out_specs=pl.BlockSpec((None, IN_PACKED, tc), lambda l, i: (l, 0, i)),
        out_shape=jax.ShapeDtypeStruct((depth, IN_PACKED, d_in), BF16),
        compiler_params=_params("parallel", "parallel"),
        name="pack_w_in",
    )(w_t)


def _lane_row(v):
    depth, n = v.shape
    return jnp.zeros((depth, 1, LANES), F32).at[:, 0, :n].set(v.astype(F32))


def kernel(x, c, positions, w_ada, b_ada, norm_mix, w_in, dn_conv, dn_a_log, dn_dt_bias, dn_norm,
           w_dn_out, swa_q_norm, swa_k_norm, swa_sinks, w_swa_out, w_o, norm_ffn, w_up, ffn_conv,
           ffn_conv_b, w_down):
    batch, seq, _ = x.shape
    depth = w_ada.shape[0]
    m = batch * seq

    mod_all = _ada_mod(c, w_ada, b_ada)
    cos_t, sin_t = _rope_tables(positions)

    w_in_p = _pack_w_in(w_in)
    w_dn_b, w_sw_b, w_o_b = w_dn_out.astype(BF16), w_swa_out.astype(BF16), w_o.astype(BF16)
    w_up_b, w_down_b = w_up.astype(BF16), w_down.astype(BF16)
    alog = _lane_row(dn_a_log)
    dtb = _lane_row(dn_dt_bias)
    qn = jnp.tile(swa_q_norm, (1, SWA_Q_HEADS)).reshape(depth, 1, SWA_WIDTH)
    kn = jnp.tile(swa_k_norm, (1, SWA_KV_HEADS)).reshape(depth, 1, SWA_KV_WIDTH)

    mod = mod_all.reshape(depth * SUBLANES * 6, 1, D_MODEL)
    norm_mix3 = norm_mix.reshape(depth, 1, D_MODEL)
    norm_ffn3 = norm_ffn.reshape(depth, 1, D_MODEL)
    dn_norm3 = dn_norm.reshape(depth, 1, DN_HEAD_DIM)
    conv_b3 = ffn_conv_b.reshape(depth, 1, D_FF)

    xf = x.reshape(m, D_MODEL)
    for l in range(depth):
        proj, gates = _inproj(xf, norm_mix3, mod, w_in_p, alog, dtb, l, seq)
        o_dn = _deltanet(proj, gates, dn_conv, dn_norm3, l, batch, seq)
        o_sw = _swa(proj, cos_t, sin_t, swa_sinks[l], qn, kn, l, batch, seq)
        xf = _merge(xf, o_dn, o_sw, proj, mod, w_dn_b, w_sw_b, w_o_b, l, seq)
        xf = _ffn(xf, norm_ffn3, mod, w_up_b, ffn_conv, conv_b3, w_down_b, l, seq)
    return xf.reshape(batch, seq, D_MODEL)
```

```python
import functools

import numpy as np
import jax
import jax.numpy as jnp
from jax import lax
from jax.experimental import pallas as pl
from jax.experimental.pallas import tpu as pltpu

F32 = jnp.float32
BF16 = jnp.bfloat16

D_MODEL = 1024
DN_HEADS = 8
DN_HEAD_DIM = 128
DN_WIDTH = DN_HEADS * DN_HEAD_DIM
DN_CONV = 4
DN_CHUNK = 64
SWA_Q_HEADS = 16
SWA_KV_HEADS = 2
SWA_HEAD_DIM = 64
SWA_WIDTH = SWA_Q_HEADS * SWA_HEAD_DIM
SWA_KV_WIDTH = SWA_KV_HEADS * SWA_HEAD_DIM
SWA_BLOCK = 128
SWA_STEP = 512
SWA_HEAD_BATCH = 16
ROPE_THETA = 500000.0
ROPE_DIM = SWA_HEAD_DIM // 4
ROPE_HALF = ROPE_DIM // 2
D_FF = 2816
FFN_CONV = 3
EPS = 1e-6

LANES = 128
SUBLANES = 8
VMEM_LIMIT = 56 * 1024 * 1024

COL_Q, COL_K, COL_V, COL_Z = 0, 1024, 2048, 3072
COL_SWQ, COL_GA, COL_GB = 4096, 5120, 6144
COL_SWK, COL_SWV, COL_AB = 7168, 7296, 7424
IN_PACKED = 7680
NEG_BIG = -1e30


def _sigmoid(x):
    return 1.0 / (1.0 + jnp.exp(-x))


def _silu(x):
    return x * _sigmoid(x)


def _mm(a, b):
    return jnp.dot(a.astype(BF16), b.astype(BF16), preferred_element_type=F32)


def _mm_nt(a, b):
    return lax.dot_general(a.astype(BF16), b.astype(BF16), (((1,), (1,)), ((), ())),
                           preferred_element_type=F32)


def _params(*sem):
    return pltpu.CompilerParams(dimension_semantics=sem, vmem_limit_bytes=VMEM_LIMIT)


def _ada_kernel(c_ref, w_ref, b_ref, o_ref):
    ca = _silu(c_ref[...])
    o_ref[0] = jnp.dot(ca, w_ref[0], precision=lax.Precision.HIGHEST,
                       preferred_element_type=F32) + b_ref[0]


def _ada_mod(c, w_ada, b_ada):
    depth = w_ada.shape[0]
    batch = c.shape[0]
    n_out = w_ada.shape[2]
    tn = 1536
    c_pad = jnp.zeros((SUBLANES, D_MODEL), F32).at[:batch].set(c)
    return pl.pallas_call(
        _ada_kernel,
        grid=(depth, n_out // tn),
        in_specs=[pl.BlockSpec((SUBLANES, D_MODEL), lambda l, j: (0, 0)),
                  pl.BlockSpec((1, D_MODEL, tn), lambda l, j: (l, 0, j)),
                  pl.BlockSpec((1, 1, tn), lambda l, j: (l, 0, j))],
        out_specs=pl.BlockSpec((1, SUBLANES, tn), lambda l, j: (l, 0, j)),
        out_shape=jax.ShapeDtypeStruct((depth, SUBLANES, n_out), F32),
        compiler_params=_params("arbitrary", "arbitrary"),
        name="ada_mod",
    )(c_pad, w_ada, b_ada.reshape(depth, 1, n_out))


def _rope_kernel(pos_ref, inv_ref, sgn_ref, cos_ref, sin_ref):
    ang = pos_ref[...] * inv_ref[...]
    on = sgn_ref[...] != 0.0
    cos_ref[...] = jnp.where(on, jnp.cos(ang), 1.0)
    sin_ref[...] = jnp.sin(ang) * sgn_ref[...]


def _rope_tables(positions):
    m = positions.size
    tm = min(2048, m)
    lane = np.arange(LANES) % SWA_HEAD_DIM
    inv = np.where(lane < ROPE_DIM,
                   np.power(ROPE_THETA, -(lane % ROPE_HALF).astype(np.float64) / ROPE_HALF), 0.0)
    sgn = np.where(lane < ROPE_HALF, -1.0, np.where(lane < ROPE_DIM, 1.0, 0.0))
    pos = positions.astype(F32).reshape(m, 1)
    return pl.pallas_call(
        _rope_kernel,
        grid=(m // tm,),
        in_specs=[pl.BlockSpec((tm, 1), lambda i: (i, 0)),
                  pl.BlockSpec((1, LANES), lambda i: (0, 0)),
                  pl.BlockSpec((1, LANES), lambda i: (0, 0))],
        out_specs=[pl.BlockSpec((tm, LANES), lambda i: (i, 0)),
                   pl.BlockSpec((tm, LANES), lambda i: (i, 0))],
        out_shape=[jax.ShapeDtypeStruct((m, LANES), F32)] * 2,
        compiler_params=_params("arbitrary"),
        name="rope_tables",
    )(pos, jnp.asarray(inv, F32).reshape(1, LANES), jnp.asarray(sgn, F32).reshape(1, LANES))


def _norm_mod(x, nw, sc, sh):
    ms = jnp.mean(x * x, axis=-1, keepdims=True)
    return (x * lax.rsqrt(ms + EPS) * nw) * (1.0 + sc) + sh


def _inproj_kernel(x_ref, nw_ref, sc_ref, sh_ref, w_ref, alog_ref, dtb_ref,
                   proj_ref, gate_ref, h_scr, *, n_col_tiles, ab_off):
    j = pl.program_id(1)

    @pl.when(j == 0)
    def _():
        h_scr[...] = _norm_mod(x_ref[...], nw_ref[...], sc_ref[0], sh_ref[0]).astype(BF16)

    acc = _mm_nt(h_scr[...], w_ref[...])
    proj_ref[...] = acc.astype(BF16)

    @pl.when(j == n_col_tiles - 1)
    def _():
        ab = acc[:, ab_off:ab_off + LANES]
        z = ab + dtb_ref[...]
        softplus = jnp.maximum(z, 0.0) + jnp.log(1.0 + jnp.exp(-jnp.abs(z)))
        g = -jnp.exp(alog_ref[...]) * softplus
        lane = lax.broadcasted_iota(jnp.int32, ab.shape, 1)
        gate_ref[...] = jnp.where(lane < DN_HEADS, g, _sigmoid(ab))


def _mod_spec(layer, k, tiles_per_seq):
    return pl.BlockSpec((1, 1, D_MODEL),
                        lambda i, *_: ((layer * SUBLANES + i // tiles_per_seq) * 6 + k, 0, 0))


def _inproj(x, nw, mod, w, alog, dtb, layer, seq):
    m = x.shape[0]
    tm, tn = 1024, 2560
    nj = IN_PACKED // tn
    tiles_per_seq = seq // tm
    kern = functools.partial(_inproj_kernel, n_col_tiles=nj, ab_off=COL_AB - (nj - 1) * tn)
    return pl.pallas_call(
        kern,
        grid=(m // tm, nj),
        in_specs=[pl.BlockSpec((tm, D_MODEL), lambda i, j: (i, 0)),
                  pl.BlockSpec((None, 1, D_MODEL), lambda i, j: (layer, 0, 0)),
                  _mod_spec(layer, 1, tiles_per_seq), _mod_spec(layer, 0, tiles_per_seq),
                  pl.BlockSpec((None, tn, D_MODEL), lambda i, j: (layer, j, 0)),
                  pl.BlockSpec((None, 1, LANES), lambda i, j: (layer, 0, 0)),
                  pl.BlockSpec((None, 1, LANES), lambda i, j: (layer, 0, 0))],
        out_specs=[pl.BlockSpec((tm, tn), lambda i, j: (i, j)),
                   pl.BlockSpec((tm, LANES), lambda i, j: (i, 0))],
        out_shape=[jax.ShapeDtypeStruct((m, IN_PACKED), BF16),
                   jax.ShapeDtypeStruct((m, LANES), F32)],
        scratch_shapes=[pltpu.VMEM((tm, D_MODEL), BF16)],
        compiler_params=_params("arbitrary", "arbitrary"),
        name="inproj",
    )(x, nw, mod, mod, w, alog, dtb)


DN_BLOCK = 512
DN_PASS_UNITS = 2
DN_UNIT = 128
INV_BLOCK = 16
DN_PREV_ROWS = 16
M_INCL, M_STRICT, M_DIAG, M_EYE, M_NEG_INCL, M_OFF0 = 0, 1, 2, 3, 4, 5
LOG2E = float(np.log2(np.e))


def _dn_masks():
    r = np.arange(DN_UNIT)[:, None]
    c = np.arange(DN_UNIT)[None, :]
    same = lambda b: (r // b) == (c // b)
    chunk = same(DN_CHUNK)
    incl = chunk & (r >= c)
    masks = [incl, chunk & (r > c), same(INV_BLOCK), r == c, np.where(incl, 0.0, NEG_BIG)]
    b = INV_BLOCK
    while b < DN_CHUNK:
        masks.append(same(2 * b) & ~same(b))
        b *= 2
    return np.stack(masks).astype(np.float32)


def _inv_unit_lower(l_strict, mask_ref, filler):
    dot = functools.partial(jnp.dot, preferred_element_type=F32)
    ds = [l * mask_ref[M_DIAG] for l in l_strict]
    ts = [mask_ref[M_EYE] - d for d in ds]
    dbs = [d.astype(BF16) for d in ds]
    mpows = [dot(db, db) for db in dbs]
    filler()
    n_fac = int(np.log2(INV_BLOCK)) - 1
    for i in range(n_fac):
        mbs = [m.astype(BF16) for m in mpows]
        ts = [t + dot(t.astype(BF16), mb) for t, mb in zip(ts, mbs)]
        if i < n_fac - 1:
            mpows = [dot(mb, mb) for mb in mbs]
        filler()
    n_levels = int(np.log2(DN_CHUNK // INV_BLOCK))
    for lvl in range(n_levels):
        tbs = [t.astype(BF16) for t in ts]
        inner = [dot((l * mask_ref[M_OFF0 + lvl]).astype(BF16), tb).astype(BF16)
                 for l, tb in zip(l_strict, tbs)]
        filler()
        ts = [t - dot(tb, inn) for t, tb, inn in zip(ts, tbs, inner)]
        filler()
    return ts


def _dn_conv_pieces(raw_refs, prev_refs, keep_prev, cw_ref, xbuf, act_ref):
    tb, halo = DN_BLOCK, DN_PREV_ROWS

    def piece(idx, h):
        sl = slice(h * DN_HEAD_DIM, (h + 1) * DN_HEAD_DIM)
        ref, pref = raw_refs[idx], prev_refs[idx]
        if pref is None:
            xbuf[idx, 0:halo, sl] = jnp.zeros((halo, DN_HEAD_DIM), F32)
        else:
            xbuf[idx, 0:halo, sl] = pref[:, sl].astype(F32) * keep_prev
        x = ref[:, sl].astype(F32)
        xbuf[idx, halo:halo + tb, sl] = x
        w = cw_ref[:, idx * DN_WIDTH + h * DN_HEAD_DIM:idx * DN_WIDTH + (h + 1) * DN_HEAD_DIM]
        y = w[DN_CONV - 1:DN_CONV, :] * x
        for s in range(1, DN_CONV):
            y = y + w[DN_CONV - 1 - s:DN_CONV - s, :] * xbuf[idx, halo - s:halo - s + tb, sl]
        y = _silu(y)
        if idx < 2:
            scale = DN_HEAD_DIM ** -0.5 if idx == 0 else 1.0
            y = y * (lax.rsqrt(jnp.sum(y * y, axis=-1, keepdims=True) + EPS) * scale)
        act_ref[idx, :, sl] = y

    return [functools.partial(piece, idx, h) for idx in range(3) for h in range(DN_HEADS)]


def _dn_intra_kernel(q_ref, k_ref, v_ref, qp_ref, kp_ref, vp_ref, gate_ref, cw_ref, mask_ref,
                     u_ref, w_ref, qd_ref, kd_ref, qk_ref, egl_ref, xbuf, act_cur, *, blocks_per_seq):
    i = pl.program_id(0)
    tb, c = DN_BLOCK, DN_CHUNK
    n_chunks = tb // c

    keep_prev = jnp.where((i % blocks_per_seq) == 0, 0.0, 1.0)
    for piece in _dn_conv_pieces((q_ref, k_ref, v_ref), (qp_ref, kp_ref, vp_ref), keep_prev,
                                 cw_ref, xbuf, act_cur):
        piece()
    pending = []

    un = DN_UNIT
    units = [slice(p * un, (p + 1) * un) for p in range(tb // un)]
    gates = gate_ref[...]
    gcum = jnp.concatenate(
        [jnp.dot(mask_ref[M_INCL], gates[rows], precision=lax.Precision.HIGHEST, preferred_element_type=F32)
         for rows in units], axis=0)
    gcum_t = gcum.T
    gcum2 = gcum * LOG2E
    gcum2_t = gcum_t * LOG2E
    glast = jnp.concatenate(
        [jnp.broadcast_to(gcum[ci * c + c - 1:ci * c + c, :], (c, LANES)) for ci in range(n_chunks)], axis=0)
    e_cum = jnp.exp(gcum)
    e_rem = jnp.exp(glast - gcum)
    for ci in range(n_chunks):
        gl = gcum_t[0:DN_HEADS, ci * c + c - 1:ci * c + c]
        egl_ref[ci * DN_HEADS:(ci + 1) * DN_HEADS, :] = jnp.broadcast_to(jnp.exp(gl), (DN_HEADS, LANES))

    def head_setup(h, part, rhs, ls):
        sl = slice(h * DN_HEAD_DIM, (h + 1) * DN_HEAD_DIM)
        span = slice(part[0].start, part[-1].stop)
        qh, kh, vh = act_cur[0, span, sl], act_cur[1, span, sl], act_cur[2, span, sl]
        beta = jnp.broadcast_to(gates[span, DN_HEADS + h:DN_HEADS + h + 1], kh.shape)
        eg = jnp.broadcast_to(e_cum[span, h:h + 1], kh.shape)
        kb = kh * beta
        qd_ref[span, sl] = (qh * eg).astype(BF16)
        kd_ref[span, sl] = (kh * e_rem[span, h:h + 1]).astype(BF16)
        rhs_h = jnp.concatenate([vh * beta, kb * eg], axis=1).astype(BF16)
        for rows in part:
            loc = slice(rows.start - span.start, rows.stop - span.start)
            rhs.append(rhs_h[loc])
            a = _mm_nt(jnp.concatenate([kb[loc], qh[loc]], axis=0), kh[loc])
            decay = jnp.exp2(gcum2[rows, h:h + 1] - gcum2_t[h:h + 1, rows] + mask_ref[M_NEG_INCL])
            ls.append(a[:un] * (decay * mask_ref[M_STRICT]))
            qk_ref[rows, sl] = (a[un:] * decay).astype(BF16)

    def filler():
        if pending:
            pending.pop(0)()

    heads = range(DN_HEADS)
    for u0 in range(0, len(units), DN_PASS_UNITS):
        part = units[u0:u0 + DN_PASS_UNITS]
        rhs, ls = [], []
        for h in heads:
            head_setup(h, part, rhs, ls)
        tinvs = _inv_unit_lower(ls, mask_ref, filler)
        where = [(h, rows) for h in heads for rows in part]
        for (h, rows), tinv, r in zip(where, tinvs, rhs):
            sl = slice(h * DN_HEAD_DIM, (h + 1) * DN_HEAD_DIM)
            uw = jnp.dot(tinv.astype(BF16), r, preferred_element_type=F32)
            u_ref[rows, sl] = uw[:, :DN_HEAD_DIM]
            w_ref[rows, sl] = uw[:, DN_HEAD_DIM:].astype(BF16)


def _dn_scan_kernel(u_ref, w_ref, qd_ref, kd_ref, qk_ref, z_ref, egl_ref, nw_ref, o_ref, s_scr, *, tb):
    t = pl.program_id(1)
    c = DN_CHUNK

    @pl.when(t == 0)
    def _():
        s_scr[...] = jnp.zeros_like(s_scr)

    nw = nw_ref[...]
    zeros_v = jnp.zeros((c, DN_HEAD_DIM), BF16)
    chains = [(bi, h) for bi in range(u_ref.shape[0]) for h in range(DN_HEADS)]
    sls = [slice(h * DN_HEAD_DIM, (h + 1) * DN_HEAD_DIM) for _, h in chains]
    dot = functools.partial(jnp.dot, preferred_element_type=F32)

    def out_matmuls(rows, ws, pads):
        return [w_s[c:] + dot(qk_ref[bi, rows, sl], v_pad)
                for (bi, _), sl, w_s, v_pad in zip(chains, sls, ws, pads)]

    def out_finish(rows, outs):
        for (bi, _), sl, o in zip(chains, sls, outs):
            o = o * lax.rsqrt(jnp.mean(o * o, axis=-1, keepdims=True) + EPS) * nw
            o_ref[bi, rows, sl] = (o * _silu(z_ref[bi, rows, sl].astype(F32))).astype(BF16)

    pending = None
    for ci in range(tb // c):
        rows = slice(ci * c, (ci + 1) * c)
        s_old = [s_scr[bi, h] for bi, h in chains]
        ws = [dot(jnp.concatenate([w_ref[bi, rows, sl], qd_ref[bi, rows, sl]], axis=0), s.astype(BF16))
              for (bi, _), sl, s in zip(chains, sls, s_old)]
        prev_outs = out_matmuls(*pending) if pending else None
        vbs = [(u_ref[bi, rows, sl] - w_s[:c]).astype(BF16) for (bi, _), sl, w_s in zip(chains, sls, ws)]
        pads = [jnp.concatenate([vb, zeros_v] if ci % 2 == 0 else [zeros_v, vb], axis=0) for vb in vbs]
        kd_ts = [kd_ref[bi, rows, sl].astype(F32).T.astype(BF16) for (bi, _), sl in zip(chains, sls)]
        for (bi, h), s, vb, kd_t in zip(chains, s_old, vbs, kd_ts):
            egl = egl_ref[bi, ci * DN_HEADS + h:ci * DN_HEADS + h + 1, :]
            s_scr[bi, h] = s * egl + dot(kd_t, vb)
        if pending:
            out_finish(pending[0], prev_outs)
        pending = (rows, ws, pads)
    out_finish(pending[0], out_matmuls(*pending))


def _deltanet(proj, gates, conv_w, norm_w, layer, batch, seq):
    m = proj.shape[0]
    tb = DN_BLOCK
    nblk = m // tb
    masks = jnp.asarray(_dn_masks())
    prev_per_blk = tb // DN_PREV_ROWS
    cols = (COL_Q // DN_WIDTH, COL_K // DN_WIDTH, COL_V // DN_WIDTH)
    cur = [pl.BlockSpec((tb, DN_WIDTH), lambda i, cb=cb: (i, cb)) for cb in cols]
    prev = [pl.BlockSpec((DN_PREV_ROWS, DN_WIDTH),
                         lambda i, cb=cb: (jnp.maximum(i * prev_per_blk - 1, 0), cb)) for cb in cols]
    tok = pl.BlockSpec((tb, DN_WIDTH), lambda i: (i, 0))
    egl_rows = (tb // DN_CHUNK) * DN_HEADS
    bf_out = jax.ShapeDtypeStruct((m, DN_WIDTH), BF16)
    u, w, qd, kd, qk, egl = pl.pallas_call(
        functools.partial(_dn_intra_kernel, blocks_per_seq=seq // tb),
        grid=(nblk,),
        in_specs=cur + prev + [
            pl.BlockSpec((tb, LANES), lambda i: (i, 0)),
            pl.BlockSpec((None, DN_CONV, 3 * DN_WIDTH), lambda i: (layer, 0, 0)),
            pl.BlockSpec(masks.shape, lambda i: (0, 0, 0))],
        out_specs=[tok, tok, tok, tok, tok, pl.BlockSpec((egl_rows, LANES), lambda i: (i, 0))],
        out_shape=[jax.ShapeDtypeStruct((m, DN_WIDTH), F32), bf_out, bf_out, bf_out, bf_out,
                   jax.ShapeDtypeStruct((m // DN_CHUNK * DN_HEADS, LANES), F32)],
        scratch_shapes=[pltpu.VMEM((3, tb + DN_PREV_ROWS, DN_WIDTH), F32),
                        pltpu.VMEM((3, tb, DN_WIDTH), F32)],
        compiler_params=_params("parallel"),
        name="dn_intra",
    )(*([proj] * 6), gates, conv_w, masks)

    ts, nseq = 512, 2
    nt = seq // ts
    blk = lambda cb: pl.BlockSpec((nseq, ts, DN_WIDTH), lambda b, t: (b, t, cb))
    seq3 = lambda a: a.reshape(batch, seq, a.shape[-1])
    egl_rows = ts // DN_CHUNK * DN_HEADS
    out = pl.pallas_call(
        functools.partial(_dn_scan_kernel, tb=ts),
        grid=(batch // nseq, nt),
        in_specs=[blk(0), blk(0), blk(0), blk(0), blk(0), blk(COL_Z // DN_WIDTH),
                  pl.BlockSpec((nseq, egl_rows, LANES), lambda b, t: (b, t, 0)),
                  pl.BlockSpec((None, 1, DN_HEAD_DIM), lambda b, t: (layer, 0, 0))],
        out_specs=blk(0),
        out_shape=jax.ShapeDtypeStruct((batch, seq, DN_WIDTH), BF16),
        scratch_shapes=[pltpu.VMEM((nseq, DN_HEADS, DN_HEAD_DIM, DN_HEAD_DIM), F32)],
        compiler_params=_params("arbitrary", "arbitrary"),
        name="dn_scan",
    )(seq3(u), seq3(w), seq3(qd), seq3(kd), seq3(qk), seq3(proj),
      egl.reshape(batch, seq // DN_CHUNK * DN_HEADS, LANES), norm_w)
    return out.reshape(m, DN_WIDTH)


SWA_TILE = 2 * LANES


def _swa_consts():
    r = np.arange(SWA_TILE)[:, None]
    c = np.arange(SWA_TILE)[None, :]
    ones = (r // SWA_HEAD_DIM) == (c // SWA_HEAD_DIM)
    cl = c % SWA_HEAD_DIM
    perm = ((cl < ROPE_HALF) & (r == c + ROPE_HALF)) | ((cl >= ROPE_HALF) & (cl < ROPE_DIM) & (r == c - ROPE_HALF))
    return np.stack([ones, perm]).astype(np.float32)


def _norm_rope_tiles(tiles, nws, coss, sins, const_ref):
    dot = functools.partial(jnp.dot, preferred_element_type=F32)
    ws = [t.shape[1] for t in tiles]
    ms = [dot((t * t).astype(BF16), const_ref[0, :w, :w]) for t, w in zip(tiles, ws)]
    ys = [t * lax.rsqrt(m * (1.0 / SWA_HEAD_DIM) + EPS) * nw for t, m, nw in zip(tiles, ms, nws)]
    partners = [dot(y.astype(BF16), const_ref[1, :w, :w]) for y, w in zip(ys, ws)]
    return [y * cs[:, :w] + p * sn[:, :w] for y, p, w, cs, sn in zip(ys, partners, ws, coss, sins)]


def _swa_kernel(sink_ref, q_ref, k_ref, v_ref, cos_ref, sin_ref, qn_ref, kn_ref, const_ref, o_ref,
                kprev, vprev):
    n = pl.program_id(1)
    blk = SWA_BLOCK

    @pl.when(n == 0)
    def _():
        kprev[...] = jnp.zeros_like(kprev)
        vprev[...] = jnp.zeros_like(vprev)

    n_qt = SWA_WIDTH // SWA_TILE
    blocks = [slice(sb * blk, (sb + 1) * blk) for sb in range(SWA_STEP // blk)]
    tiles, nws, coss, sins = [], [], [], []
    for rows in blocks:
        cos_t = jnp.tile(cos_ref[rows, :], (1, SWA_TILE // LANES))
        sin_t = jnp.tile(sin_ref[rows, :], (1, SWA_TILE // LANES))
        tiles += [q_ref[rows, t * SWA_TILE:(t + 1) * SWA_TILE].astype(F32) for t in range(n_qt)]
        tiles.append(k_ref[rows, :].astype(F32))
        nws += [qn_ref[:, t * SWA_TILE:(t + 1) * SWA_TILE] for t in range(n_qt)] + [kn_ref[...]]
        coss += [cos_t] * (n_qt + 1)
        sins += [sin_t] * (n_qt + 1)
    roped = _norm_rope_tiles(tiles, nws, coss, sins, const_ref)

    for sb, rows in enumerate(blocks):
        mine = roped[sb * (n_qt + 1):(sb + 1) * (n_qt + 1)]
        qb = jnp.concatenate([(t * (SWA_HEAD_DIM ** -0.5 * LOG2E)).astype(BF16) for t in mine[:n_qt]], axis=1)
        prev_penalty = jnp.where(n > 0, 0.0, NEG_BIG) if sb == 0 else 0.0
        _swa_block(rows, prev_penalty, qb, mine[n_qt], sink_ref, v_ref, o_ref, kprev, vprev)


def _swa_block(rows, prev_penalty, qb, k_roped, sink_ref, v_ref, o_ref, kprev, vprev):
    blk = SWA_BLOCK

    low = lax.broadcasted_iota(jnp.int32, (blk, LANES), 1) < SWA_HEAD_DIM

    def split_heads(cur, prev_ref):
        swap = pltpu.roll(cur, SWA_HEAD_DIM, axis=1)
        parts = [jnp.where(low, cur, 0.0), jnp.where(low, swap, 0.0),
                 jnp.where(low, 0.0, swap), jnp.where(low, 0.0, cur)]
        bands = []
        for idx, part in enumerate(parts):
            part = part.astype(BF16)
            bands.append(jnp.concatenate([prev_ref[idx], part], axis=0))
            prev_ref[idx] = part
        return bands

    k_bands = split_heads(k_roped, kprev)
    v_bands = split_heads(v_ref[rows, :].astype(F32), vprev)

    qi = lax.broadcasted_iota(jnp.int32, (blk, 2 * blk), 0)
    kj = lax.broadcasted_iota(jnp.int32, (blk, 2 * blk), 1)
    rel = qi + blk - kj
    bias = jnp.where(rel >= 0, jnp.where(rel < blk, 0.0, NEG_BIG), NEG_BIG)
    bias = bias + jnp.where(kj < blk, prev_penalty, 0.0)

    group = SWA_Q_HEADS // SWA_KV_HEADS
    for h0 in range(0, SWA_Q_HEADS, SWA_HEAD_BATCH):
        heads = range(h0, h0 + SWA_HEAD_BATCH)
        kops = [k_bands[2 * (h % 2) + h // group] for h in heads]
        vops = [v_bands[2 * (h % 2) + h // group] for h in heads]
        sinks = [sink_ref[h] * LOG2E for h in heads]
        ss = [_mm_nt(qb[:, (h // 2) * LANES:(h // 2 + 1) * LANES], kop) + bias
              for h, kop in zip(heads, kops)]
        mxs = [jnp.maximum(jnp.max(s, axis=-1, keepdims=True), sink) for s, sink in zip(ss, sinks)]
        ps = [jnp.exp2(s - mx) for s, mx in zip(ss, mxs)]
        denoms = [jnp.sum(p, axis=-1, keepdims=True) + jnp.exp2(sink - mx)
                  for p, sink, mx in zip(ps, sinks, mxs)]
        outs = [_mm(p, vop) * (1.0 / d) for p, vop, d in zip(ps, vops, denoms)]
        for pair in range(h0 // 2, (h0 + SWA_HEAD_BATCH) // 2):
            o_ref[rows, pair * LANES:(pair + 1) * LANES] = (
                outs[2 * pair - h0] + outs[2 * pair + 1 - h0]).astype(BF16)


def _swa(proj, cos_t, sin_t, sinks, qn, kn, layer, batch, seq):
    m = proj.shape[0]
    blk, step = SWA_BLOCK, SWA_STEP
    nb = seq // step
    row = lambda b, n: b * nb + n
    consts = jnp.asarray(_swa_consts(), BF16)
    return pl.pallas_call(
        _swa_kernel,
        grid=(batch, nb),
        in_specs=[pl.BlockSpec(memory_space=pltpu.SMEM),
                  pl.BlockSpec((step, SWA_WIDTH), lambda b, n: (row(b, n), COL_SWQ // SWA_WIDTH)),
                  pl.BlockSpec((step, SWA_KV_WIDTH), lambda b, n: (row(b, n), COL_SWK // SWA_KV_WIDTH)),
                  pl.BlockSpec((step, SWA_KV_WIDTH), lambda b, n: (row(b, n), COL_SWV // SWA_KV_WIDTH)),
                  pl.BlockSpec((step, LANES), lambda b, n: (row(b, n), 0)),
                  pl.BlockSpec((step, LANES), lambda b, n: (row(b, n), 0)),
                  pl.BlockSpec((None, 1, SWA_WIDTH), lambda b, n: (layer, 0, 0)),
                  pl.BlockSpec((None, 1, SWA_KV_WIDTH), lambda b, n: (layer, 0, 0)),
                  pl.BlockSpec(consts.shape, lambda b, n: (0, 0, 0))],
        out_specs=pl.BlockSpec((step, SWA_WIDTH), lambda b, n: (row(b, n), 0)),
        out_shape=jax.ShapeDtypeStruct((m, SWA_WIDTH), BF16),
        scratch_shapes=[pltpu.VMEM((2 * SWA_KV_HEADS, blk, SWA_KV_WIDTH), BF16),
                        pltpu.VMEM((2 * SWA_KV_HEADS, blk, SWA_KV_WIDTH), BF16)],
        compiler_params=_params("arbitrary", "arbitrary"),
        name="swa",
    )(sinks, proj, proj, proj, cos_t, sin_t, qn, kn, consts)


def _merge_kernel(x_ref, odn_ref, osw_ref, ga_ref, gb_ref, gt_ref, wdn_ref, wsw_ref, wo_ref, o_ref):
    ya = jnp.dot(odn_ref[...], wdn_ref[...], preferred_element_type=F32)
    yb = jnp.dot(osw_ref[...], wsw_ref[...], preferred_element_type=F32)
    merged = _sigmoid(ga_ref[...].astype(F32)) * ya + _sigmoid(gb_ref[...].astype(F32)) * yb
    out = jnp.dot(merged.astype(BF16), wo_ref[...], preferred_element_type=F32)
    o_ref[...] = x_ref[...] + gt_ref[0] * out


def _merge(x, o_dn, o_sw, proj, mod, w_dn, w_sw, w_o, layer, seq):
    m = x.shape[0]
    tm = 1024
    tiles_per_seq = seq // tm
    tok = lambda cb: pl.BlockSpec((tm, D_MODEL), lambda i: (i, cb))
    wfull = pl.BlockSpec((None, D_MODEL, D_MODEL), lambda i: (layer, 0, 0))
    return pl.pallas_call(
        _merge_kernel,
        grid=(m // tm,),
        in_specs=[tok(0), tok(0), tok(0), tok(COL_GA // D_MODEL), tok(COL_GB // D_MODEL),
                  _mod_spec(layer, 2, tiles_per_seq), wfull, wfull, wfull],
        out_specs=tok(0),
        out_shape=jax.ShapeDtypeStruct((m, D_MODEL), F32),
        compiler_params=_params("arbitrary"),
        name="merge_out",
    )(x, o_dn, o_sw, proj, proj, mod, w_dn, w_sw, w_o)


def _ffn_kernel(x_ref, nw_ref, sc_ref, sh_ref, gt_ref, wa_ref, wl_ref, cw_ref, cb_ref, wd_ref,
                o_ref, h_scr, acc_scr, abuf, halo_scr, *, tm, tiles_per_seq, n_ff_tiles):
    i = pl.program_id(0)
    j = pl.program_id(1)
    halo = SUBLANES

    @pl.when(j == 0)
    def _():
        h_scr[...] = _norm_mod(x_ref[...], nw_ref[...], sc_ref[0], sh_ref[0]).astype(BF16)

    h = h_scr[...]
    a = jnp.dot(h, wa_ref[...], preferred_element_type=F32)
    lin = jnp.dot(h, wl_ref[...], preferred_element_type=F32)

    first = (i % tiles_per_seq) == 0
    prev = halo_scr[j]
    abuf[0:halo, :] = jnp.where(first, jnp.zeros_like(prev), prev)
    abuf[halo:halo + tm, :] = a
    halo_scr[j] = a[tm - halo:tm, :]
    w = cw_ref[...]
    y = w[FFN_CONV - 1:FFN_CONV, :] * a + cb_ref[...]
    for s in range(1, FFN_CONV):
        y = y + w[FFN_CONV - 1 - s:FFN_CONV - s, :] * abuf[halo - s:halo - s + tm, :]
    act = (_silu(y) * lin).astype(BF16)
    part = jnp.dot(act, wd_ref[...], preferred_element_type=F32)

    @pl.when(j == 0)
    def _():
        acc_scr[...] = part

    @pl.when(j > 0)
    def _():
        acc_scr[...] += part

    @pl.when(j == n_ff_tiles - 1)
    def _():
        o_ref[...] = x_ref[...] + gt_ref[0] * acc_scr[...]


def _ffn(x, nw, mod, w_up, conv_w, conv_b, w_down, layer, seq):
    m = x.shape[0]
    tm = 1024
    n_ff = 2
    fc = D_FF // n_ff
    tiles_per_seq = seq // tm
    kern = functools.partial(_ffn_kernel, tm=tm, tiles_per_seq=tiles_per_seq, n_ff_tiles=n_ff)
    return pl.pallas_call(
        kern,
        grid=(m // tm, n_ff),
        in_specs=[pl.BlockSpec((tm, D_MODEL), lambda i, j: (i, 0)),
                  pl.BlockSpec((None, 1, D_MODEL), lambda i, j: (layer, 0, 0)),
                  _mod_spec(layer, 4, tiles_per_seq), _mod_spec(layer, 3, tiles_per_seq),
                  _mod_spec(layer, 5, tiles_per_seq),
                  pl.BlockSpec((None, D_MODEL, fc), lambda i, j: (layer, 0, j)),
                  pl.BlockSpec((None, D_MODEL, fc), lambda i, j: (layer, 0, n_ff + j)),
                  pl.BlockSpec((None, FFN_CONV, fc), lambda i, j: (layer, 0, j)),
                  pl.BlockSpec((None, 1, fc), lambda i, j: (layer, 0, j)),
                  pl.BlockSpec((None, fc, D_MODEL), lambda i, j: (layer, j, 0))],
        out_specs=pl.BlockSpec((tm, D_MODEL), lambda i, j: (i, 0)),
        out_shape=jax.ShapeDtypeStruct((m, D_MODEL), F32),
        scratch_shapes=[pltpu.VMEM((tm, D_MODEL), BF16),
                        pltpu.VMEM((tm, D_MODEL), F32),
                        pltpu.VMEM((tm + SUBLANES, fc), F32),
                        pltpu.VMEM((n_ff, SUBLANES, fc), F32)],
        compiler_params=_params("arbitrary", "arbitrary"),
        name="ffn",
    )(x, nw, mod, mod, mod, w_up, w_up, conv_w, conv_b, w_down)


def _pack_moves():
    o_a = 4 * DN_WIDTH
    o_swq = o_a + 2 * DN_HEADS
    o_swk = o_swq + SWA_WIDTH
    o_swv = o_swk + SWA_KV_WIDTH
    o_ga = o_swv + SWA_KV_WIDTH
    o_gb = o_ga + D_MODEL
    return ((0, COL_Q, o_a), (o_swq, COL_SWQ, SWA_WIDTH), (o_ga, COL_GA, D_MODEL), (o_gb, COL_GB, D_MODEL),
            (o_swk, COL_SWK, SWA_KV_WIDTH), (o_swv, COL_SWV, SWA_KV_WIDTH), (o_a, COL_AB, 2 * DN_HEADS))


def _pack_kernel(w_ref, o_ref):
    for src, dst, width in _pack_moves():
        o_ref[dst:dst + width, :] = w_ref[src:src + width, :].astype(BF16)
    tail = COL_AB + 2 * DN_HEADS
    o_ref[tail:, :] = jnp.zeros((IN_PACKED - tail, o_ref.shape[1]), BF16)


def _pack_w_in(w_in):
    depth, d_in, n_in = w_in.shape
    w_t = jnp.swapaxes(w_in, 1, 2)
    tc = 256
    return pl.pallas_call(
        _pack_kernel,
        grid=(depth, d_in // tc),
        in_specs=[pl.BlockSpec((None, n_in, tc), lambda l, i: (l, 0, i))],
        out_specs=pl.BlockSpec((None, IN_PACKED, tc), lambda l, i: (l, 0, i)),
        out_shape=jax.ShapeDtypeStruct((depth, IN_PACKED, d_in), BF16),
        compiler_params=_params("parallel", "parallel"),
        name="pack_w_in",
    )(w_t)


def _lane_row(v):
    depth, n = v.shape
    return jnp.zeros((depth, 1, LANES), F32).at[:, 0, :n].set(v.astype(F32))


def kernel(x, c, positions, w_ada, b_ada, norm_mix, w_in, dn_conv, dn_a_log, dn_dt_bias, dn_norm,
           w_dn_out, swa_q_norm, swa_k_norm, swa_sinks, w_swa_out, w_o, norm_ffn, w_up, ffn_conv,
           ffn_conv_b, w_down):
    batch, seq, _ = x.shape
    depth = w_ada.shape[0]
    m = batch * seq

    mod_all = _ada_mod(c, w_ada, b_ada)
    cos_t, sin_t = _rope_tables(positions)

    w_in_p = _pack_w_in(w_in)
    w_dn_b, w_sw_b, w_o_b = w_dn_out.astype(BF16), w_swa_out.astype(BF16), w_o.astype(BF16)
    w_up_b, w_down_b = w_up.astype(BF16), w_down.astype(BF16)
    alog = _lane_row(dn_a_log)
    dtb = _lane_row(dn_dt_bias)
    qn = jnp.tile(swa_q_norm, (1, SWA_Q_HEADS)).reshape(depth, 1, SWA_WIDTH)
    kn = jnp.tile(swa_k_norm, (1, SWA_KV_HEADS)).reshape(depth, 1, SWA_KV_WIDTH)

    mod = mod_all.reshape(depth * SUBLANES * 6, 1, D_MODEL)
    norm_mix3 = norm_mix.reshape(depth, 1, D_MODEL)
    norm_ffn3 = norm_ffn.reshape(depth, 1, D_MODEL)
    dn_norm3 = dn_norm.reshape(depth, 1, DN_HEAD_DIM)
    conv_b3 = ffn_conv_b.reshape(depth, 1, D_FF)

    xf = x.reshape(m, D_MODEL)
    for l in range(depth):
        proj, gates = _inproj(xf, norm_mix3, mod, w_in_p, alog, dtb, l, seq)
        o_dn = _deltanet(proj, gates, dn_conv, dn_norm3, l, batch, seq)
        o_sw = _swa(proj, cos_t, sin_t, swa_sinks[l], qn, kn, l, batch, seq)
        xf = _merge(xf, o_dn, o_sw, proj, mod, w_dn_b, w_sw_b, w_o_b, l, seq)
        xf = _ffn(xf, norm_ffn3, mod, w_up_b, ffn_conv, conv_b3, w_down_b, l, seq)
    return xf.reshape(batch, seq, D_MODEL)
```
